```python
import numpy as np
import jax
import jax.numpy as jnp
from jax import lax

D_MODEL = 1024
BATCH = 8
SEQ = 2048
DEPTH = 2

D_MIX = D_MODEL
MLSTM_HEADS = 4
MLSTM_HEAD_DIM = D_MIX // 8
SB_HEADS = 4
SB_HEAD_DIM = D_MIX // 16
POOL_GROUPS = 4
POOL_CH = D_MIX // 16
POOL_WINDOWS = (2, 4, 8, 16)
W_M = MLSTM_HEADS * MLSTM_HEAD_DIM
W_SB = SB_HEADS * SB_HEAD_DIM
W_P = POOL_GROUPS * POOL_CH
MLSTM_CHUNK = 64
MLSTM_CONV = 4
SB_BLOCK = 128
FFN_CONV = 3
D_FF = ((8 * D_MODEL // 3 + 127) // 128) * 128
EPS = 1e-6
SPLIT_SIZES = (W_M, W_M, W_M, W_M, MLSTM_HEADS, MLSTM_HEADS, W_SB, W_SB, W_SB, W_P)
D_IN = sum(SPLIT_SIZES)

kernel_name = 'hybrid_mlstm_stickbreak_pool_block'


def rmsnorm(x, g):
    x32 = x.astype(jnp.float32)
    y = x32 * lax.rsqrt(jnp.mean(x32 * x32, axis=-1, keepdims=True) + EPS)
    return (y * g.astype(jnp.float32)).astype(x.dtype)


def causal_dwconv(x, w, b):
    k_width = w.shape[0]
    s = x.shape[1]
    xp = jnp.pad(x, ((0, 0), (k_width - 1, 0), (0, 0)))
    y = b
    for j in range(k_width):
        y = y + xp[:, j:j + s] * w[j]
    return y


def _mlstm_chunk(carry, xs):
    c_mat, n_vec, m_prev = carry
    q, k, v, log_f, i_log = xs
    L = q.shape[2]
    causal = jnp.tril(jnp.ones((L, L), dtype=bool))
    b = jnp.cumsum(log_f, axis=-1)
    d = jnp.where(causal, b[..., :, None] - b[..., None, :] + i_log[..., None, :], -jnp.inf)
    inter = b + m_prev[..., None]
    m_t = jnp.maximum(inter, jnp.max(d, axis=-1))
    w_inter = jnp.exp(inter - m_t)
    scores = jnp.einsum('bhtd,bhsd->bhts', q, k) * jnp.exp(d - m_t[..., None])
    num = (w_inter[..., None] * jnp.einsum('bhtd,bhde->bhte', q, c_mat)
           + jnp.einsum('bhts,bhse->bhte', scores, v))
    den = w_inter * jnp.einsum('bhtd,bhd->bht', q, n_vec) + jnp.sum(scores, axis=-1)
    h = num / jnp.maximum(jnp.abs(den), jnp.exp(-m_t))[..., None]
    b_end = b[..., -1]
    g = b_end[..., None] - b + i_log
    m_new = jnp.maximum(b_end + m_prev, jnp.max(g, axis=-1))
    decay = jnp.exp(b_end + m_prev - m_new)
    w_s = jnp.exp(g - m_new[..., None])
    c_new = decay[..., None, None] * c_mat + jnp.einsum('bhs,bhsd,bhse->bhde', w_s, k, v)
    n_new = decay[..., None] * n_vec + jnp.einsum('bhs,bhsd->bhd', w_s, k)
    return (c_new, n_new, m_new), h


def mlstm(q, k, v, i_pre, f_pre):
    bsz, s, h, dh = q.shape
    nc = s // MLSTM_CHUNK

    def to_chunks(a):
        a = a.astype(jnp.float32).reshape((bsz, nc, MLSTM_CHUNK, h) + a.shape[3:])
        return jnp.moveaxis(a, (1, 3), (0, 2))

    xs = (to_chunks(q), to_chunks(k) * (dh ** -0.5), to_chunks(v),
          to_chunks(jax.nn.log_sigmoid(f_pre.astype(jnp.float32))), to_chunks(i_pre))
    init = (jnp.zeros((bsz, h, dh, dh), jnp.float32),
            jnp.zeros((bsz, h, dh), jnp.float32),
            jnp.zeros((bsz, h), jnp.float32))
    _, hs = lax.scan(_mlstm_chunk, init, xs)
    hs = jnp.moveaxis(hs, (0, 2), (1, 3))
    return hs.reshape(bsz, s, h * dh)


def stick_breaking(q, k, v):
    bsz, s, h, d = q.shape
    qh = jnp.transpose(q, (0, 2, 1, 3)).astype(jnp.float32) * (d ** -0.5)
    kh = jnp.transpose(k, (0, 2, 1, 3)).astype(jnp.float32)
    vh = jnp.transpose(v, (0, 2, 1, 3)).astype(jnp.float32)
    s_idx = jnp.arange(s)

    def block(i):
        start = i * SB_BLOCK
        qb = lax.dynamic_slice_in_dim(qh, start, SB_BLOCK, axis=2)
        z = jnp.einsum('bhtd,bhsd->bhts', qb, kh)
        t_idx = start + jnp.arange(SB_BLOCK)
        mask = s_idx[None, :] < t_idx[:, None]
        log_1mb = jnp.where(mask, jax.nn.log_sigmoid(-z), 0.0)
        after = lax.cumsum(log_1mb, axis=3, reverse=True) - log_1mb
        a = jnp.where(mask, jnp.exp(jax.nn.log_sigmoid(z) + after), 0.0)
        return jnp.einsum('bhts,bhsd->bhtd', a, vh)

    out = lax.map(block, jnp.arange(s // SB_BLOCK))
    out = jnp.transpose(out, (1, 0, 3, 2, 4))
    return out.reshape(bsz, s, h * d)


def pool_mixer(u, pool_w, pool_scale):
    bsz, s, _ = u.shape
    ug = u.astype(jnp.float32).reshape(bsz, s, POOL_GROUPS, POOL_CH)
    cs = jnp.cumsum(ug, axis=1)
    pos = jnp.arange(s, dtype=jnp.float32)
    outs = []
    for g, w in enumerate(POOL_WINDOWS):
        c = cs[:, :, g]
        lag = jnp.pad(c, ((0, 0), (w, 0), (0, 0)))[:, :s]
        cnt = jnp.minimum(pos + 1.0, float(w))[None, :, None]
        outs.append((c - lag) / cnt - ug[:, :, g])
    y = jnp.stack(outs, axis=2)
    y = jnp.einsum('bsgc,gce->bsge', y, pool_w.astype(jnp.float32))
    return (y.reshape(bsz, s, W_P) * pool_scale).astype(u.dtype)


def setup_inputs(seed: int = 0) -> dict:
    key = jax.random.key(seed)
    ks = jax.random.split(key, 21)
    nrm = jax.random.normal

    def gain(k, n):
        return 1.0 + 0.05 * nrm(k, (DEPTH, n), jnp.float32)

    return {
        'x': nrm(ks[0], (BATCH, SEQ, D_MODEL), jnp.float32),
        'pre_mix_g': gain(ks[1], D_MODEL),
        'w_in': nrm(ks[2], (DEPTH, D_MODEL, D_IN), jnp.float32) * D_MODEL ** -0.5,
        'mlstm_conv_w': nrm(ks[3], (DEPTH, MLSTM_CONV, 2 * W_M), jnp.float32) * MLSTM_CONV ** -0.5,
        'mlstm_conv_b': 0.01 * nrm(ks[4], (DEPTH, 2 * W_M), jnp.float32),
        'i_bias': 0.1 * nrm(ks[5], (DEPTH, MLSTM_HEADS), jnp.float32),
        'f_bias': jnp.linspace(3.0, 6.0, MLSTM_HEADS)[None, :] + 0.1 * nrm(ks[6], (DEPTH, MLSTM_HEADS), jnp.float32),
        'pool_w': nrm(ks[7], (DEPTH, POOL_GROUPS, POOL_CH, POOL_CH), jnp.float32) * POOL_CH ** -0.5,
        'pool_scale': 1.0 + 0.1 * nrm(ks[8], (DEPTH, W_P), jnp.float32),
        'mlstm_out_g': gain(ks[9], W_M),
        'sb_out_g': gain(ks[10], W_SB),
        'pool_out_g': gain(ks[11], W_P),
        'w_out': nrm(ks[12], (DEPTH, D_MIX, D_MODEL), jnp.float32) * D_MIX ** -0.5,
        'post_mix_g': gain(ks[13], D_MODEL),
        'pre_ffn_g': gain(ks[14], D_MODEL),
        'ffn_w_up': nrm(ks[15], (DEPTH, D_MODEL, 2 * D_FF), jnp.float32) * D_MODEL ** -0.5,
        'ffn_conv_w': nrm(ks[16], (DEPTH, FFN_CONV, 2 * D_FF), jnp.float32) * FFN_CONV ** -0.5,
        'ffn_conv_b': 0.01 * nrm(ks[17], (DEPTH, 2 * D_FF), jnp.float32),
        'ffn_w_down': nrm(ks[18], (DEPTH, D_FF, D_MODEL), jnp.float32) * D_FF ** -0.5,
        'post_ffn_g': gain(ks[19], D_MODEL),
    }


def reference(x, pre_mix_g, w_in, mlstm_conv_w, mlstm_conv_b, i_bias, f_bias, pool_w, pool_scale,
              mlstm_out_g, sb_out_g, pool_out_g, w_out, post_mix_g, pre_ffn_g, ffn_w_up,
              ffn_conv_w, ffn_conv_b, ffn_w_down, post_ffn_g):
    bsz, s, _ = x.shape
    offsets = np.cumsum(SPLIT_SIZES)[:-1].tolist()
    for l in range(DEPTH):
        h = rmsnorm(x, pre_mix_g[l])
        proj = jnp.einsum('bsd,de->bse', h, w_in[l])
        q_m, k_m, v_m, o_m, i_m, f_m, q_sb, k_sb, v_sb, u_p = jnp.split(proj, offsets, axis=-1)
        qk = jax.nn.silu(causal_dwconv(jnp.concatenate([q_m, k_m], axis=-1), mlstm_conv_w[l], mlstm_conv_b[l]))
        q_m, k_m = jnp.split(qk, 2, axis=-1)
        hm = (bsz, s, MLSTM_HEADS, MLSTM_HEAD_DIM)
        y_m = mlstm(q_m.reshape(hm), k_m.reshape(hm), v_m.reshape(hm), i_m + i_bias[l], f_m + f_bias[l])
        y_m = (jax.nn.sigmoid(o_m.astype(jnp.float32)) * y_m).astype(x.dtype)
        hs = (bsz, s, SB_HEADS, SB_HEAD_DIM)
        y_sb = stick_breaking(q_sb.reshape(hs), k_sb.reshape(hs), v_sb.reshape(hs)).astype(x.dtype)
        y_p = pool_mixer(u_p, pool_w[l], pool_scale[l])
        mix = jnp.concatenate([rmsnorm(y_m, mlstm_out_g[l]), rmsnorm(y_sb, sb_out_g[l]),
                               rmsnorm(y_p, pool_out_g[l])], axis=-1)
        x = x + rmsnorm(jnp.einsum('bse,ed->bsd', mix, w_out[l]), post_mix_g[l])
        h = rmsnorm(x, pre_ffn_g[l])
        up = causal_dwconv(jnp.einsum('bsd,df->bsf', h, ffn_w_up[l]), ffn_conv_w[l], ffn_conv_b[l])
        gate, val = jnp.split(up, 2, axis=-1)
        ffn = jnp.einsum('bsf,fd->bsd', jax.nn.gelu(gate, approximate=True) * val, ffn_w_down[l])
        x = x + rmsnorm(ffn, post_ffn_g[l])
    return x
```

```python
import functools

import numpy as np
import jax
import jax.numpy as jnp
from jax import lax
from jax.experimental import pallas as pl
from jax.experimental.pallas import tpu as pltpu

F32 = jnp.float32
BF16 = jnp.bfloat16

D_MODEL = 1024
MLSTM_HEADS = 4
MLSTM_HEAD_DIM = 128
SB_HEADS = 4
SB_HEAD_DIM = 64
POOL_GROUPS = 4
POOL_CH = 64
POOL_WINDOWS = (2, 4, 8, 16)
W_M = MLSTM_HEADS * MLSTM_HEAD_DIM
W_SB = SB_HEADS * SB_HEAD_DIM
W_P = POOL_GROUPS * POOL_CH
MLSTM_CONV = 4
FFN_CONV = 3
D_FF = 2816
EPS = 1e-6

LANES = 128
SUBLANES = 8
GATE_W = LANES
QK_HALO = SUBLANES
POOL_HALO = 2 * SUBLANES
FFN_HALO = SUBLANES

TM_IN = 512
TM_FFN = 512
MLSTM_L = 64
SB_TQ = 256
SB_TK = 256
FFN_FC = 256
VMEM_LIMIT = 56 * 1024 * 1024


def _rms(x, g):
    return x * lax.rsqrt(jnp.mean(x * x, axis=-1, keepdims=True) + EPS) * g


def _sigmoid(x):
    return 1.0 / (1.0 + jnp.exp(-x))


def _log_sigmoid(x):
    return jnp.minimum(x, 0.0) - jnp.log1p(jnp.exp(-jnp.abs(x)))


def _const_spec(shape):
    nd = len(shape)
    return pl.BlockSpec(shape, lambda *_: (0,) * nd, pipeline_mode=pl.Buffered(1))


def _inproj_kernel(x_ref, g_ref, wm_ref, wg_ref, cw_ref, cb_ref, gb_ref, pw_ref, ps_ref, pg_ref,
                   q_ref, k_ref, v_ref, o_ref, gate_ref, qsb_ref, ksb_ref, vsb_ref, yp_ref,
                   qk_buf, u_buf, *, tiles_per_seq):
    tm = x_ref.shape[0]
    tile = pl.program_id(0) % tiles_per_seq
    h = _rms(x_ref[...], g_ref[...]).astype(BF16)

    @pl.when(tile == 0)
    def _():
        qk_buf[0:QK_HALO, :] = jnp.zeros((QK_HALO, 2 * W_M), F32)
        u_buf[0:POOL_HALO, :] = jnp.zeros((POOL_HALO, W_P), F32)

    @pl.when(tile != 0)
    def _():
        qk_buf[0:QK_HALO, :] = qk_buf[tm:tm + QK_HALO, :]
        u_buf[0:POOL_HALO, :] = u_buf[tm:tm + POOL_HALO, :]

    c0 = 0
    qk_buf[QK_HALO:QK_HALO + tm, :] = jnp.dot(h, wm_ref[:, c0:c0 + 2 * W_M], preferred_element_type=F32)
    c0 += 2 * W_M
    acc = cb_ref[...]
    for j in range(MLSTM_CONV):
        off = QK_HALO - (MLSTM_CONV - 1) + j
        acc = acc + cw_ref[j:j + 1, :] * qk_buf[off:off + tm, :]
    qk = acc * _sigmoid(acc)
    q_ref[...] = qk[:, :W_M].astype(BF16)
    k_ref[...] = (qk[:, W_M:] * (MLSTM_HEAD_DIM ** -0.5)).astype(BF16)

    v_ref[...] = jnp.dot(h, wm_ref[:, c0:c0 + W_M], preferred_element_type=F32).astype(BF16)
    c0 += W_M
    o_ref[...] = jnp.dot(h, wm_ref[:, c0:c0 + W_M], preferred_element_type=F32).astype(BF16)
    c0 += W_M

    gates = jnp.dot(h, wg_ref[...], preferred_element_type=F32) + gb_ref[...]
    lane = lax.broadcasted_iota(jnp.int32, gates.shape, 1)
    gate_ref[...] = jnp.where(lane < MLSTM_HEADS, gates, _log_sigmoid(gates))

    sb = jnp.dot(h, wm_ref[:, c0:c0 + 3 * W_SB], preferred_element_type=F32)
    c0 += 3 * W_SB
    for hh in range(SB_HEADS):
        lo = hh * SB_HEAD_DIM
        qsb_ref[hh] = (sb[:, lo:lo + SB_HEAD_DIM] * (SB_HEAD_DIM ** -0.5)).astype(BF16)
        ksb_ref[hh] = sb[:, W_SB + lo:W_SB + lo + SB_HEAD_DIM].astype(BF16)
        vsb_ref[hh] = sb[:, 2 * W_SB + lo:2 * W_SB + lo + SB_HEAD_DIM].astype(BF16)

    u = jnp.dot(h, wm_ref[:, c0:c0 + W_P], preferred_element_type=F32)
    u_buf[POOL_HALO:POOL_HALO + tm, :] = u
    ext = u_buf[...]
    sums = []
    s = ext
    for shift in (1, 2, 4, 8):
        s = s + pltpu.roll(s, shift, 0)
        sums.append(s[POOL_HALO:, :])
    grp = lax.broadcasted_iota(jnp.int32, (tm, W_P), 1) // POOL_CH
    pos = (tile * tm + lax.broadcasted_iota(jnp.int32, (tm, W_P), 0)).astype(F32)
    win = sums[-1]
    width = jnp.full((tm, W_P), float(POOL_WINDOWS[-1]), F32)
    for gi in range(POOL_GROUPS - 2, -1, -1):
        win = jnp.where(grp == gi, sums[gi], win)
        width = jnp.where(grp == gi, float(POOL_WINDOWS[gi]), width)
    y = win / jnp.minimum(pos + 1.0, width) - u
    yp = jnp.dot(y.astype(BF16), pw_ref[...], preferred_element_type=F32) * ps_ref[...]
    yp_ref[...] = _rms(yp, pg_ref[...]).astype(BF16)


def _inproj(x, g, wm, wg, cw, cb, gb, pw, ps, pg, *, seq):
    n = x.shape[0]
    tm = TM_IN
    tok = lambda w: pl.BlockSpec((tm, w), lambda i: (i, 0))
    head = pl.BlockSpec((SB_HEADS, tm, SB_HEAD_DIM), lambda i: (0, i, 0))
    sds = jax.ShapeDtypeStruct
    return pl.pallas_call(
        functools.partial(_inproj_kernel, tiles_per_seq=seq // tm),
        grid=(n // tm,),
        in_specs=[tok(D_MODEL), _const_spec(g.shape), _const_spec(wm.shape), _const_spec(wg.shape),
                  _const_spec(cw.shape), _const_spec(cb.shape), _const_spec(gb.shape), _const_spec(pw.shape),
                  _const_spec(ps.shape), _const_spec(pg.shape)],
        out_specs=[tok(W_M), tok(W_M), tok(W_M), tok(W_M), tok(GATE_W), head, head, head, tok(W_P)],
        out_shape=[sds((n, W_M), BF16), sds((n, W_M), BF16), sds((n, W_M), BF16), sds((n, W_M), BF16),
                   sds((n, GATE_W), F32),
                   sds((SB_HEADS, n, SB_HEAD_DIM), BF16), sds((SB_HEADS, n, SB_HEAD_DIM), BF16),
                   sds((SB_HEADS, n, SB_HEAD_DIM), BF16), sds((n, W_P), BF16)],
        scratch_shapes=[pltpu.VMEM((tm + QK_HALO, 2 * W_M), F32), pltpu.VMEM((tm + POOL_HALO, W_P), F32)],
        compiler_params=pltpu.CompilerParams(dimension_semantics=("arbitrary",), vmem_limit_bytes=VMEM_LIMIT),
        name="inproj",
    )(x, g, wm, wg, cw, cb, gb, pw, ps, pg)


def _mlstm_kernel(q_ref, k_ref, v_ref, o_ref, gc_ref, gr_ref, og_ref, y_ref, c_ref, m_ref):
    L = q_ref.shape[0]
    dh = MLSTM_HEAD_DIM

    @pl.when(pl.program_id(1) == 0)
    def _():
        c_ref[...] = jnp.zeros(c_ref.shape, F32)
        m_ref[...] = jnp.zeros(m_ref.shape, F32)

    gc = gc_ref[...]
    gr = gr_ref[0]
    row = lax.broadcasted_iota(jnp.int32, (L, L), 0)
    col = lax.broadcasted_iota(jnp.int32, (L, L), 1)
    causal = col <= row
    b_cols = jnp.dot(causal.astype(F32), gc, precision=lax.Precision.HIGHEST, preferred_element_type=F32)
    b_rows = jnp.dot(gr, (row <= col).astype(F32), precision=lax.Precision.HIGHEST, preferred_element_type=F32)
    ones = jnp.ones((L, dh), BF16)
    ys = []
    for h in range(MLSTM_HEADS):
        hs = slice(h * dh, (h + 1) * dh)
        f = MLSTM_HEADS + h
        b_c = b_cols[:, f:f + 1]
        i_c = gc[:, h:h + 1]
        b_r = b_rows[f:f + 1, :]
        i_r = gr[h:h + 1, :]
        m_prev = m_ref[h:h + 1, 0:1]
        d = jnp.where(causal, b_c - b_r + i_r, -jnp.inf)
        inter = b_c + m_prev
        m_t = jnp.maximum(inter, jnp.max(d, axis=1, keepdims=True))
        w_inter = jnp.exp(inter - m_t)
        q = q_ref[:, hs]
        k = k_ref[:, hs]
        v_aug = jnp.concatenate([v_ref[:, hs], ones], axis=1)
        c_aug = c_ref[h]
        scores = lax.dot_general(q, k, (((1,), (1,)), ((), ())), preferred_element_type=F32) * jnp.exp(d - m_t)
        hv = (jnp.dot(scores.astype(BF16), v_aug, preferred_element_type=F32)
              + w_inter * jnp.dot(q, c_aug.astype(BF16), preferred_element_type=F32))
        hh = hv[:, :dh] / jnp.maximum(jnp.abs(hv[:, dh:]), jnp.exp(-m_t))
        ys.append(_sigmoid(o_ref[:, hs].astype(F32)) * hh)
        b_end = b_c[L - 1:L, :]
        g = b_end - b_c + i_c
        m_new = jnp.maximum(b_end + m_prev, jnp.max(g, axis=0, keepdims=True))
        decay = jnp.exp(b_end + m_prev - m_new)
        kw = (k.astype(F32) * jnp.exp(g - m_new)).astype(BF16)
        c_ref[h] = decay * c_aug + lax.dot_general(kw, v_aug, (((0,), (0,)), ((), ())),
                                                   preferred_element_type=F32)
        m_ref[h:h + 1, :] = jnp.broadcast_to(m_new, (1, LANES))
    y_ref[...] = _rms(jnp.concatenate(ys, axis=1), og_ref[...]).astype(BF16)


def _mlstm(q, k, v, o, gates, og, *, batch, seq):
    n = q.shape[0]
    L = MLSTM_L
    nc = seq // L
    gr = jnp.swapaxes(gates[:, :2 * MLSTM_HEADS].reshape(n // L, L, 2 * MLSTM_HEADS), 1, 2)
    tok = lambda w: pl.BlockSpec((L, w), lambda b, t: (b * nc + t, 0))
    return pl.pallas_call(
        _mlstm_kernel,
        grid=(batch, nc),
        in_specs=[tok(W_M), tok(W_M), tok(W_M), tok(W_M), tok(GATE_W),
                  pl.BlockSpec((1, 2 * MLSTM_HEADS, L), lambda b, t: (b * nc + t, 0, 0)),
                  pl.BlockSpec(og.shape, lambda b, t: (0, 0))],
        out_specs=tok(W_M),
        out_shape=jax.ShapeDtypeStruct((n, W_M), BF16),
        scratch_shapes=[pltpu.VMEM((MLSTM_HEADS, MLSTM_HEAD_DIM, 2 * MLSTM_HEAD_DIM), F32),
                        pltpu.VMEM((SUBLANES, LANES), F32)],
        compiler_params=pltpu.CompilerParams(dimension_semantics=("arbitrary", "arbitrary"),
                                             vmem_limit_bytes=VMEM_LIMIT),
        name="mlstm",
    )(q, k, v, o, gates, gr, og)


def _sb_kernel(q_ref, k_ref, v_ref, mm_ref, o_ref):
    tq = q_ref.shape[1]
    tk = SB_TK
    i = pl.program_id(2)
    q = q_ref[0]
    row = lax.broadcasted_iota(jnp.int32, (tq, tk), 0)
    col = lax.broadcasted_iota(jnp.int32, (tq, tk), 1)
    valid = col < row

    def block(j, carry, acc, masked):
        start = pl.multiple_of(j * tk, tk)
        kb = k_ref[0, pl.ds(start, tk), :]
        vb = v_ref[0, pl.ds(start, tk), :]
        z = lax.dot_general(q, kb, (((1,), (1,)), ((), ())), preferred_element_type=F32)
        nl = jnp.maximum(z, 0.0) + jnp.log1p(jnp.exp(-jnp.abs(z)))
        if masked:
            nl = jnp.where(valid, nl, 0.0)
        hi = nl.astype(BF16)
        lo = (nl - hi.astype(F32)).astype(BF16)
        cs = jnp.dot(jnp.concatenate([hi, lo], axis=1), mm_ref[...], preferred_element_type=F32)
        after = cs[:, :tk] + jnp.concatenate([carry] * (tk // LANES), axis=1)
        a = jnp.exp(z - nl - after)
        if masked:
            a = jnp.where(valid, a, 0.0)
        acc = acc + jnp.dot(a.astype(BF16), vb, preferred_element_type=F32)
        return carry + cs[:, tk:], acc

    carry = jnp.zeros((tq, LANES), F32)
    acc = jnp.zeros((tq, SB_HEAD_DIM), F32)
    carry, acc = block(i, carry, acc, True)
    carry, acc = lax.fori_loop(0, i, lambda n, c: block(i - 1 - n, c[0], c[1], False), (carry, acc))
    o_ref[0] = acc.astype(BF16)


def _sb_matrix():
    tk = SB_TK
    later = (np.arange(tk)[:, None] > np.arange(tk)[None, :]).astype(np.float32)
    one = np.concatenate([later, np.ones((tk, LANES), np.float32)], axis=1)
    return jnp.asarray(np.concatenate([one, one], axis=0), BF16)


def _sb(q, k, v, *, batch, seq):
    n = q.shape[1]
    assert SB_TQ == SB_TK
    nq = seq // SB_TQ
    mm = _sb_matrix()
    return pl.pallas_call(
        _sb_kernel,
        grid=(batch, SB_HEADS, nq),
        in_specs=[pl.BlockSpec((1, SB_TQ, SB_HEAD_DIM), lambda b, h, i: (h, b * nq + i, 0)),
                  pl.BlockSpec((1, seq, SB_HEAD_DIM), lambda b, h, i: (h, b, 0)),
                  pl.BlockSpec((1, seq, SB_HEAD_DIM), lambda b, h, i: (h, b, 0)),
                  pl.BlockSpec(mm.shape, lambda b, h, i: (0, 0))],
        out_specs=pl.BlockSpec((1, SB_TQ, SB_HEAD_DIM), lambda b, h, i: (h, b * nq + i, 0)),
        out_shape=jax.ShapeDtypeStruct((SB_HEADS, n, SB_HEAD_DIM), BF16),
        compiler_params=pltpu.CompilerParams(dimension_semantics=("arbitrary", "arbitrary", "arbitrary"),
                                             vmem_limit_bytes=VMEM_LIMIT),
        name="stickbreak",
    )(q, k, v, mm)


def _outffn_kernel(x_ref, ym_ref, ysb_ref, yp_ref, sbg_ref, wo_ref, pmg_ref, pfg_ref,
                   wug_ref, wuv_ref, cwg_ref, cwv_ref, cbg_ref, cbv_ref, wd_ref, pog_ref,
                   out_ref, h_buf, act_buf, up_buf, halo_buf, *, tiles_per_seq):
    tm = x_ref.shape[0]
    nchunk = wug_ref.shape[0]
    fc = wug_ref.shape[2]
    tile = pl.program_id(0) % tiles_per_seq

    ysb = jnp.concatenate([ysb_ref[hh].astype(F32) for hh in range(SB_HEADS)], axis=1)
    mix = jnp.concatenate([ym_ref[...], _rms(ysb, sbg_ref[...]).astype(BF16), yp_ref[...]], axis=1)
    x1 = x_ref[...] + _rms(jnp.dot(mix, wo_ref[...], preferred_element_type=F32), pmg_ref[...])
    h_buf[...] = _rms(x1, pfg_ref[...]).astype(BF16)

    @pl.when(tile == 0)
    def _():
        halo_buf[...] = jnp.zeros(halo_buf.shape, F32)

    def conv_half(c, half, w_ref, cw_ref, cb_ref):
        up_buf[0:FFN_HALO, :] = halo_buf[2 * c + half]
        up = jnp.dot(h_buf[...], w_ref[c], preferred_element_type=F32)
        up_buf[FFN_HALO:FFN_HALO + tm, :] = up
        halo_buf[2 * c + half] = up[tm - FFN_HALO:, :]
        cw = cw_ref[c]
        acc = cb_ref[c]
        for j in range(FFN_CONV):
            off = FFN_HALO - (FFN_CONV - 1) + j
            acc = acc + cw[j:j + 1, :] * up_buf[off:off + tm, :]
        return acc

    def chunk(c, carry):
        gate = conv_half(c, 0, wug_ref, cwg_ref, cbg_ref)
        val = conv_half(c, 1, wuv_ref, cwv_ref, cbv_ref)
        act_buf[c] = (jax.nn.gelu(gate, approximate=True) * val).astype(BF16)
        return carry

    lax.fori_loop(0, nchunk, chunk, 0)
    ffn = jnp.dot(act_buf[0], wd_ref[0], preferred_element_type=F32)
    for c in range(1, nchunk):
        ffn = ffn + jnp.dot(act_buf[c], wd_ref[c], preferred_element_type=F32)
    out_ref[...] = x1 + _rms(ffn, pog_ref[...])


def _outffn(x, ym, ysb, yp, sbg, wo, pmg, pfg, wug, wuv, cwg, cwv, cbg, cbv, wd, pog, *, seq):
    n = x.shape[0]
    tm = TM_FFN
    nchunk, _, fc = wug.shape
    tok = lambda w: pl.BlockSpec((tm, w), lambda i: (i, 0))
    consts = (sbg, wo, pmg, pfg, wug, wuv, cwg, cwv, cbg, cbv, wd, pog)
    return pl.pallas_call(
        functools.partial(_outffn_kernel, tiles_per_seq=seq // tm),
        grid=(n // tm,),
        in_specs=[tok(D_MODEL), tok(W_M), pl.BlockSpec((SB_HEADS, tm, SB_HEAD_DIM), lambda i: (0, i, 0)), tok(W_P)]
                 + [_const_spec(c.shape) for c in consts],
        out_specs=tok(D_MODEL),
        out_shape=jax.ShapeDtypeStruct((n, D_MODEL), F32),
        scratch_shapes=[pltpu.VMEM((tm, D_MODEL), BF16), pltpu.VMEM((nchunk, tm, fc), BF16),
                        pltpu.VMEM((tm + FFN_HALO, fc), F32), pltpu.VMEM((2 * nchunk, FFN_HALO, fc), F32)],
        compiler_params=pltpu.CompilerParams(dimension_semantics=("arbitrary",), vmem_limit_bytes=VMEM_LIMIT),
        name="outffn",
    )(x, ym, ysb, yp, *consts)


def _chunked_cols(w, fc):
    rows = w.shape[0]
    return jnp.swapaxes(w.reshape(rows, D_FF // fc, fc), 0, 1)


def kernel(x, pre_mix_g, w_in, mlstm_conv_w, mlstm_conv_b, i_bias, f_bias, pool_w, pool_scale, mlstm_out_g,
           sb_out_g, pool_out_g, w_out, post_mix_g, pre_ffn_g, ffn_w_up, ffn_conv_w, ffn_conv_b, ffn_w_down,
           post_ffn_g):
    batch, seq, d_model = x.shape
    depth = w_in.shape[0]
    assert d_model == D_MODEL and ffn_w_down.shape[1] == D_FF
    assert seq % TM_IN == 0 and seq % TM_FFN == 0 and seq % MLSTM_L == 0 and seq % SB_TQ == 0
    n = batch * seq
    xf = x.reshape(n, d_model)
    row = lambda a: a.reshape(1, -1).astype(F32)
    o_gate = 4 * W_M
    o_sb = o_gate + 2 * MLSTM_HEADS
    for l in range(depth):
        w = w_in[l]
        wm = jnp.concatenate([w[:, :o_gate], w[:, o_sb:]], axis=1).astype(BF16)
        wg = jnp.pad(w[:, o_gate:o_sb], ((0, 0), (0, GATE_W - 2 * MLSTM_HEADS))).astype(BF16)
        gb = jnp.pad(jnp.concatenate([i_bias[l], f_bias[l]]), (0, GATE_W - 2 * MLSTM_HEADS)).reshape(1, GATE_W)
        pw = jax.scipy.linalg.block_diag(*[pool_w[l, g] for g in range(POOL_GROUPS)]).astype(BF16)
        q, k, v, o, gates, qsb, ksb, vsb, yp = _inproj(
            xf, row(pre_mix_g[l]), wm, wg, mlstm_conv_w[l], row(mlstm_conv_b[l]), gb, pw,
            row(pool_scale[l]), row(pool_out_g[l]), seq=seq)
        ym = _mlstm(q, k, v, o, gates, row(mlstm_out_g[l]), batch=batch, seq=seq)
        ysb = _sb(qsb, ksb, vsb, batch=batch, seq=seq)
        fc = FFN_FC
        up_w, cw, cb = ffn_w_up[l], ffn_conv_w[l], ffn_conv_b[l].reshape(1, -1)
        xf = _outffn(
            xf, ym, ysb, yp, row(sb_out_g[l]), w_out[l].astype(BF16), row(post_mix_g[l]), row(pre_ffn_g[l]),
            _chunked_cols(up_w[:, :D_FF], fc).astype(BF16), _chunked_cols(up_w[:, D_FF:], fc).astype(BF16),
            _chunked_cols(cw[:, :D_FF], fc), _chunked_cols(cw[:, D_FF:], fc),
            _chunked_cols(cb[:, :D_FF], fc), _chunked_cols(cb[:, D_FF:], fc),
            ffn_w_down[l].reshape(D_FF // fc, fc, D_MODEL).astype(BF16), row(post_ffn_g[l]), seq=seq)
    return xf.reshape(batch, seq, d_model)
```

```python
import functools

import numpy as np
import jax
import jax.numpy as jnp
from jax import lax
from jax.experimental import pallas as pl
from jax.experimental.pallas import tpu as pltpu

F32 = jnp.float32
BF16 = jnp.bfloat16

D_MODEL = 1024
MLSTM_HEADS = 4
MLSTM_HEAD_DIM = 128
SB_HEADS = 4
SB_HEAD_DIM = 64
POOL_GROUPS = 4
POOL_CH = 64
POOL_WINDOWS = (2, 4, 8, 16)
W_M = MLSTM_HEADS * MLSTM_HEAD_DIM
W_SB = SB_HEADS * SB_HEAD_DIM
W_P = POOL_GROUPS * POOL_CH
MLSTM_CONV = 4
FFN_CONV = 3
D_FF = 2816
EPS = 1e-6
LOG2E = 1.4426950408889634

LANES = 128
SUBLANES = 8
GATE_W = LANES
QK_HALO = SUBLANES
POOL_HALO = 2 * SUBLANES
FFN_HALO = SUBLANES

TM_IN = 512
TM_FFN = 512
MLSTM_L = 256
SB_TQ = 256
SB_TK = 256
FFN_FC = 256
VMEM_LIMIT = 56 * 1024 * 1024


def _rms(x, g):
    return x * lax.rsqrt(jnp.mean(x * x, axis=-1, keepdims=True) + EPS) * g


def _sigmoid(x):
    return 1.0 / (1.0 + jnp.exp(-x))


def _log_sigmoid(x):
    return jnp.minimum(x, 0.0) - jnp.log1p(jnp.exp(-jnp.abs(x)))


def _const_spec(shape):
    nd = len(shape)
    return pl.BlockSpec(shape, lambda *_: (0,) * nd, pipeline_mode=pl.Buffered(1))


def _inproj_kernel(x_ref, g_ref, wm_ref, wg_ref, cw_ref, cb_ref, gb_ref, pw_ref, ps_ref, pg_ref,
                   q_ref, k_ref, v_ref, o_ref, gate_ref, qsb_ref, ksb_ref, vsb_ref, yp_ref,
                   qk_buf, u_buf, *, tiles_per_seq):
    tm = x_ref.shape[0]
    tile = pl.program_id(0) % tiles_per_seq
    h = _rms(x_ref[...], g_ref[...]).astype(BF16)

    @pl.when(tile == 0)
    def _():
        qk_buf[0:QK_HALO, :] = jnp.zeros((QK_HALO, 2 * W_M), F32)
        u_buf[0:POOL_HALO, :] = jnp.zeros((POOL_HALO, W_P), F32)

    @pl.when(tile != 0)
    def _():
        qk_buf[0:QK_HALO, :] = qk_buf[tm:tm + QK_HALO, :]
        u_buf[0:POOL_HALO, :] = u_buf[tm:tm + POOL_HALO, :]

    c0 = 0
    qk_buf[QK_HALO:QK_HALO + tm, :] = jnp.dot(h, wm_ref[:, c0:c0 + 2 * W_M], preferred_element_type=F32)
    c0 += 2 * W_M
    acc = cb_ref[...]
    for j in range(MLSTM_CONV):
        off = QK_HALO - (MLSTM_CONV - 1) + j
        acc = acc + cw_ref[j:j + 1, :] * qk_buf[off:off + tm, :]
    qk = acc * _sigmoid(acc)
    q_ref[...] = qk[:, :W_M].astype(BF16)
    k_ref[...] = (qk[:, W_M:] * (MLSTM_HEAD_DIM ** -0.5)).astype(BF16)

    v_ref[...] = jnp.dot(h, wm_ref[:, c0:c0 + W_M], preferred_element_type=F32).astype(BF16)
    c0 += W_M
    o_ref[...] = jnp.dot(h, wm_ref[:, c0:c0 + W_M], preferred_element_type=F32).astype(BF16)
    c0 += W_M

    gates = jnp.dot(h, wg_ref[...], preferred_element_type=F32) + gb_ref[...]
    lane = lax.broadcasted_iota(jnp.int32, gates.shape, 1)
    gate_ref[...] = jnp.where(lane < MLSTM_HEADS, gates, _log_sigmoid(gates))

    sb = jnp.dot(h, wm_ref[:, c0:c0 + 3 * W_SB], preferred_element_type=F32)
    c0 += 3 * W_SB
    for hh in range(SB_HEADS):
        lo = hh * SB_HEAD_DIM
        qsb_ref[hh] = (sb[:, lo:lo + SB_HEAD_DIM] * (SB_HEAD_DIM ** -0.5 * LOG2E)).astype(BF16)
        ksb_ref[hh] = sb[:, W_SB + lo:W_SB + lo + SB_HEAD_DIM].astype(BF16)
        vsb_ref[hh] = sb[:, 2 * W_SB + lo:2 * W_SB + lo + SB_HEAD_DIM].astype(BF16)

    u = jnp.dot(h, wm_ref[:, c0:c0 + W_P], preferred_element_type=F32)
    u_buf[POOL_HALO:POOL_HALO + tm, :] = u
    ext = u_buf[...]
    sums = []
    s = ext
    for shift in (1, 2, 4, 8):
        s = s + pltpu.roll(s, shift, 0)
        sums.append(s[POOL_HALO:, :])
    grp = lax.broadcasted_iota(jnp.int32, (tm, W_P), 1) // POOL_CH
    pos = (tile * tm + lax.broadcasted_iota(jnp.int32, (tm, W_P), 0)).astype(F32)
    win = sums[-1]
    width = jnp.full((tm, W_P), float(POOL_WINDOWS[-1]), F32)
    for gi in range(POOL_GROUPS - 2, -1, -1):
        win = jnp.where(grp == gi, sums[gi], win)
        width = jnp.where(grp == gi, float(POOL_WINDOWS[gi]), width)
    y = win / jnp.minimum(pos + 1.0, width) - u
    yp = jnp.dot(y.astype(BF16), pw_ref[...], preferred_element_type=F32) * ps_ref[...]
    yp_ref[...] = _rms(yp, pg_ref[...]).astype(BF16)


def _inproj(x, g, wm, wg, cw, cb, gb, pw, ps, pg, *, seq):
    n = x.shape[0]
    tm = TM_IN
    tok = lambda w: pl.BlockSpec((tm, w), lambda i: (i, 0))
    head = pl.BlockSpec((SB_HEADS, tm, SB_HEAD_DIM), lambda i: (0, i, 0))
    sds = jax.ShapeDtypeStruct
    return pl.pallas_call(
        functools.partial(_inproj_kernel, tiles_per_seq=seq // tm),
        grid=(n // tm,),
        in_specs=[tok(D_MODEL), _const_spec(g.shape), _const_spec(wm.shape), _const_spec(wg.shape),
                  _const_spec(cw.shape), _const_spec(cb.shape), _const_spec(gb.shape), _const_spec(pw.shape),
                  _const_spec(ps.shape), _const_spec(pg.shape)],
        out_specs=[tok(W_M), tok(W_M), tok(W_M), tok(W_M), tok(GATE_W), head, head, head, tok(W_P)],
        out_shape=[sds((n, W_M), BF16), sds((n, W_M), BF16), sds((n, W_M), BF16), sds((n, W_M), BF16),
                   sds((n, GATE_W), F32),
                   sds((SB_HEADS, n, SB_HEAD_DIM), BF16), sds((SB_HEADS, n, SB_HEAD_DIM), BF16),
                   sds((SB_HEADS, n, SB_HEAD_DIM), BF16), sds((n, W_P), BF16)],
        scratch_shapes=[pltpu.VMEM((tm + QK_HALO, 2 * W_M), F32), pltpu.VMEM((tm + POOL_HALO, W_P), F32)],
        compiler_params=pltpu.CompilerParams(dimension_semantics=("arbitrary",), vmem_limit_bytes=VMEM_LIMIT),
        name="inproj",
    )(x, g, wm, wg, cw, cb, gb, pw, ps, pg)


def _split3(x):
    hi = x.astype(BF16)
    r = x - hi.astype(F32)
    mid = r.astype(BF16)
    return hi, mid, (r - mid.astype(F32)).astype(BF16)


def _mlstm_kernel(q_ref, k_ref, v_ref, o_ref, gc_ref, gr_ref, og_ref, tl_ref, tu_ref, y_ref, c_ref, m_ref):
    L = q_ref.shape[0]
    dh = MLSTM_HEAD_DIM
    heads = range(MLSTM_HEADS)

    @pl.when(pl.program_id(1) == 0)
    def _():
        c_ref[...] = jnp.zeros(c_ref.shape, F32)
        m_ref[...] = jnp.zeros(m_ref.shape, F32)

    gc = gc_ref[...]
    gr = gr_ref[0]
    causal = lax.broadcasted_iota(jnp.int32, (L, L), 1) <= lax.broadcasted_iota(jnp.int32, (L, L), 0)
    b_cols = jnp.dot(tl_ref[...], jnp.concatenate(_split3(gc), axis=0), preferred_element_type=F32)
    b_rows = jnp.dot(jnp.concatenate(_split3(gr), axis=1), tu_ref[...], preferred_element_type=F32)
    ones = jnp.ones((L, dh), BF16)
    hs = [slice(h * dh, (h + 1) * dh) for h in heads]

    qk, qc, v_aug, c_aug = [], [], [], []
    for h in heads:
        q = q_ref[:, hs[h]]
        v_aug.append(jnp.concatenate([v_ref[:, hs[h]], ones], axis=1))
        c_aug.append(c_ref[h])
        qk.append(lax.dot_general(q, k_ref[:, hs[h]], (((1,), (1,)), ((), ())), preferred_element_type=F32))
        qc.append(jnp.dot(q, c_aug[h].astype(BF16), preferred_element_type=F32))

    hv, cols = [], []
    for h in heads:
        f = MLSTM_HEADS + h
        b_c = b_cols[:, f:f + 1]
        i_c = gc[:, h:h + 1]
        w_r = gr[h:h + 1, :] - b_rows[f:f + 1, :]
        m_prev = m_ref[h:h + 1, 0:1]
        cm = jnp.max(jnp.where(causal, w_r, -jnp.inf), axis=1, keepdims=True)
        inter = b_c + m_prev
        m_t = jnp.maximum(inter, b_c + cm)
        dexp = jnp.exp2(jnp.where(causal, (b_c - m_t) * LOG2E + w_r * LOG2E, -jnp.inf))
        hv.append(jnp.dot((qk[h] * dexp).astype(BF16), v_aug[h], preferred_element_type=F32))
        cols.append((b_c, i_c, cm, inter, m_t, m_prev))

    ys = []
    for h in heads:
        b_c, i_c, cm, inter, m_t, m_prev = cols[h]
        tot = hv[h] + jnp.exp(inter - m_t) * qc[h]
        hh = tot[:, :dh] / jnp.maximum(jnp.abs(tot[:, dh:]), jnp.exp(-m_t))
        ys.append(_sigmoid(o_ref[:, hs[h]].astype(F32)) * hh)
        b_end = b_c[L - 1:L, :]
        m_new = jnp.maximum(b_end + m_prev, b_end + cm[L - 1:L, :])
        decay = jnp.exp(b_end + m_prev - m_new)
        kw = (k_ref[:, hs[h]].astype(F32) * jnp.exp(b_end - b_c + i_c - m_new)).astype(BF16)
        c_ref[h] = decay * c_aug[h] + lax.dot_general(kw, v_aug[h], (((0,), (0,)), ((), ())),
                                                      preferred_element_type=F32)
        m_ref[h:h + 1, :] = jnp.broadcast_to(m_new, (1, LANES))
    y_ref[...] = _rms(jnp.concatenate(ys, axis=1), og_ref[...]).astype(BF16)


def _mlstm(q, k, v, o, gates, og, *, batch, seq):
    n = q.shape[0]
    L = MLSTM_L
    nc = seq // L
    gr = jnp.swapaxes(gates[:, :2 * MLSTM_HEADS].reshape(n // L, L, 2 * MLSTM_HEADS), 1, 2)
    lower = (np.arange(L)[None, :] <= np.arange(L)[:, None]).astype(np.float32)
    tl = jnp.asarray(np.concatenate([lower] * 3, axis=1), BF16)
    tu = jnp.asarray(np.concatenate([lower.T] * 3, axis=0), BF16)
    tok = lambda w: pl.BlockSpec((L, w), lambda b, t: (b * nc + t, 0))
    return pl.pallas_call(
        _mlstm_kernel,
        grid=(batch, nc),
        in_specs=[tok(W_M), tok(W_M), tok(W_M), tok(W_M), tok(GATE_W),
                  pl.BlockSpec((1, 2 * MLSTM_HEADS, L), lambda b, t: (b * nc + t, 0, 0)),
                  pl.BlockSpec(og.shape, lambda b, t: (0, 0)),
                  pl.BlockSpec(tl.shape, lambda b, t: (0, 0)), pl.BlockSpec(tu.shape, lambda b, t: (0, 0))],
        out_specs=tok(W_M),
        out_shape=jax.ShapeDtypeStruct((n, W_M), BF16),
        scratch_shapes=[pltpu.VMEM((MLSTM_HEADS, MLSTM_HEAD_DIM, 2 * MLSTM_HEAD_DIM), F32),
                        pltpu.VMEM((SUBLANES, LANES), F32)],
        compiler_params=pltpu.CompilerParams(dimension_semantics=("arbitrary", "arbitrary"),
                                             vmem_limit_bytes=VMEM_LIMIT),
        name="mlstm",
    )(q, k, v, o, gates, gr, og, tl, tu)


def _sb_kernel(q_ref, k_ref, v_ref, mm_ref, o_ref):
    nh, tq, _ = q_ref.shape
    tk = SB_TK
    i = pl.program_id(1)
    row = lax.broadcasted_iota(jnp.int32, (tq, tk), 0)
    col = lax.broadcasted_iota(jnp.int32, (tq, tk), 1)
    valid = col < row

    def step(j, state, masked):
        start = pl.multiple_of(j * tk, tk)
        zs, lbs, firsts, css, out = [], [], [], [], []
        for h in range(nh):
            kb = k_ref[h, pl.ds(start, tk), :]
            zs.append(lax.dot_general(q_ref[h], kb, (((1,), (1,)), ((), ())), preferred_element_type=F32))
        for h in range(nh):
            z = zs[h]
            nl = jnp.maximum(z, 0.0) + jnp.log(1.0 + jnp.exp2(-jnp.abs(z))) * LOG2E
            lbs.append(z - nl)
            if masked:
                nl = jnp.where(valid, nl, 0.0)
            hi = nl.astype(BF16)
            lo = (nl - hi.astype(F32)).astype(BF16)
            firsts.append(nl[:, 0:1])
            css.append(jnp.dot(jnp.concatenate([hi, lo], axis=1), mm_ref[...], preferred_element_type=F32))
        for h in range(nh):
            carry, acc = state[h]
            after = css[h] + jnp.concatenate([carry] * (tk // LANES), axis=1)
            a = jnp.exp2(lbs[h] - after)
            if masked:
                a = jnp.where(valid, a, 0.0)
            vb = v_ref[h, pl.ds(start, tk), :]
            acc = acc + jnp.dot(a.astype(BF16), vb, preferred_element_type=F32)
            total = css[h][:, 0:1] + firsts[h]
            out.append((carry + jnp.broadcast_to(total, (tq, LANES)), acc))
        return tuple(out)

    state = tuple((jnp.zeros((tq, LANES), F32), jnp.zeros((tq, SB_HEAD_DIM), F32)) for _ in range(nh))
    state = step(i, state, True)
    state = lax.fori_loop(0, i, lambda n, st: step(i - 1 - n, st, False), state)
    for h in range(nh):
        o_ref[h] = state[h][1].astype(BF16)


def _sb_matrix():
    tk = SB_TK
    later = (np.arange(tk)[:, None] > np.arange(tk)[None, :]).astype(np.float32)
    return jnp.asarray(np.concatenate([later, later], axis=0), BF16)


def _sb(q, k, v, *, batch, seq):
    n = q.shape[1]
    assert SB_TQ == SB_TK
    nq = seq // SB_TQ
    mm = _sb_matrix()
    return pl.pallas_call(
        _sb_kernel,
        grid=(batch, nq),
        in_specs=[pl.BlockSpec((SB_HEADS, SB_TQ, SB_HEAD_DIM), lambda b, i: (0, b * nq + i, 0)),
                  pl.BlockSpec((SB_HEADS, seq, SB_HEAD_DIM), lambda b, i: (0, b, 0)),
                  pl.BlockSpec((SB_HEADS, seq, SB_HEAD_DIM), lambda b, i: (0, b, 0)),
                  pl.BlockSpec(mm.shape, lambda b, i: (0, 0))],
        out_specs=pl.BlockSpec((SB_HEADS, SB_TQ, SB_HEAD_DIM), lambda b, i: (0, b * nq + i, 0)),
        out_shape=jax.ShapeDtypeStruct((SB_HEADS, n, SB_HEAD_DIM), BF16),
        compiler_params=pltpu.CompilerParams(dimension_semantics=("arbitrary", "arbitrary"),
                                             vmem_limit_bytes=VMEM_LIMIT),
        name="stickbreak",
    )(q, k, v, mm)


def _outffn_kernel(x_ref, ym_ref, ysb_ref, yp_ref, sbg_ref, wo_ref, pmg_ref, pfg_ref,
                   wug_ref, wuv_ref, cwg_ref, cwv_ref, cbg_ref, cbv_ref, wd_ref, pog_ref,
                   out_ref, h_buf, act_buf, up_buf, halo_buf, *, tiles_per_seq):
    tm = x_ref.shape[0]
    nchunk = wug_ref.shape[0]
    fc = wug_ref.shape[2]
    tile = pl.program_id(0) % tiles_per_seq

    ysb = jnp.concatenate([ysb_ref[hh].astype(F32) for hh in range(SB_HEADS)], axis=1)
    mix = jnp.concatenate([ym_ref[...], _rms(ysb, sbg_ref[...]).astype(BF16), yp_ref[...]], axis=1)
    x1 = x_ref[...] + _rms(jnp.dot(mix, wo_ref[...], preferred_element_type=F32), pmg_ref[...])
    h_buf[...] = _rms(x1, pfg_ref[...]).astype(BF16)

    @pl.when(tile == 0)
    def _():
        halo_buf[...] = jnp.zeros(halo_buf.shape, F32)

    def up_dots(c):
        hb = h_buf[...]
        return (jnp.dot(hb, wug_ref[c], preferred_element_type=F32),
                jnp.dot(hb, wuv_ref[c], preferred_element_type=F32))

    def conv_half(c, half, up, cw_ref, cb_ref):
        buf = up_buf.at[2 * (c % 2) + half]
        buf[0:FFN_HALO, :] = halo_buf[2 * c + half]
        buf[FFN_HALO:FFN_HALO + tm, :] = up
        halo_buf[2 * c + half] = up[tm - FFN_HALO:, :]
        cw = cw_ref[c]
        acc = cb_ref[c]
        for j in range(FFN_CONV):
            off = FFN_HALO - (FFN_CONV - 1) + j
            acc = acc + cw[j:j + 1, :] * buf[off:off + tm, :]
        return acc

    ups = up_dots(0)
    for c in range(nchunk):
        nxt = up_dots(c + 1) if c + 1 < nchunk else None
        gate = conv_half(c, 0, ups[0], cwg_ref, cbg_ref)
        val = conv_half(c, 1, ups[1], cwv_ref, cbv_ref)
        act_buf[c] = (jax.nn.gelu(gate, approximate=True) * val).astype(BF16)
        ups = nxt
    ffn = jnp.dot(act_buf[0], wd_ref[0], preferred_element_type=F32)
    for c in range(1, nchunk):
        ffn = ffn + jnp.dot(act_buf[c], wd_ref[c], preferred_element_type=F32)
    out_ref[...] = x1 + _rms(ffn, pog_ref[...])


def _outffn(x, ym, ysb, yp, sbg, wo, pmg, pfg, wug, wuv, cwg, cwv, cbg, cbv, wd, pog, *, seq):
    n = x.shape[0]
    tm = TM_FFN
    nchunk, _, fc = wug.shape
    tok = lambda w: pl.BlockSpec((tm, w), lambda i: (i, 0))
    consts = (sbg, wo, pmg, pfg, wug, wuv, cwg, cwv, cbg, cbv, wd, pog)
    return pl.pallas_call(
        functools.partial(_outffn_kernel, tiles_per_seq=seq // tm),
        grid=(n // tm,),
        in_specs=[tok(D_MODEL), tok(W_M), pl.BlockSpec((SB_HEADS, tm, SB_HEAD_DIM), lambda i: (0, i, 0)), tok(W_P)]
                 + [_const_spec(c.shape) for c in consts],
        out_specs=tok(D_MODEL),
        out_shape=jax.ShapeDtypeStruct((n, D_MODEL), F32),
        scratch_shapes=[pltpu.VMEM((tm, D_MODEL), BF16), pltpu.VMEM((nchunk, tm, fc), BF16),
                        pltpu.VMEM((4, tm + FFN_HALO, fc), F32), pltpu.VMEM((2 * nchunk, FFN_HALO, fc), F32)],
        compiler_params=pltpu.CompilerParams(dimension_semantics=("arbitrary",), vmem_limit_bytes=VMEM_LIMIT),
        name="outffn",
    )(x, ym, ysb, yp, *consts)


def _chunked_cols(w, fc):
    rows = w.shape[0]
    return jnp.swapaxes(w.reshape(rows, D_FF // fc, fc), 0, 1)


def kernel(x, pre_mix_g, w_in, mlstm_conv_w, mlstm_conv_b, i_bias, f_bias, pool_w, pool_scale, mlstm_out_g,
           sb_out_g, pool_out_g, w_out, post_mix_g, pre_ffn_g, ffn_w_up, ffn_conv_w, ffn_conv_b, ffn_w_down,
           post_ffn_g):
    batch, seq, d_model = x.shape
    depth = w_in.shape[0]
    assert d_model == D_MODEL and ffn_w_down.shape[1] == D_FF
    assert seq % TM_IN == 0 and seq % TM_FFN == 0 and seq % MLSTM_L == 0 and seq % SB_TQ == 0
    n = batch * seq
    xf = x.reshape(n, d_model)
    row = lambda a: a.reshape(1, -1).astype(F32)
    o_gate = 4 * W_M
    o_sb = o_gate + 2 * MLSTM_HEADS
    for l in range(depth):
        w = w_in[l]
        wm = jnp.concatenate([w[:, :o_gate], w[:, o_sb:]], axis=1).astype(BF16)
        wg = jnp.pad(w[:, o_gate:o_sb], ((0, 0), (0, GATE_W - 2 * MLSTM_HEADS))).astype(BF16)
        gb = jnp.pad(jnp.concatenate([i_bias[l], f_bias[l]]), (0, GATE_W - 2 * MLSTM_HEADS)).reshape(1, GATE_W)
        pw = jax.scipy.linalg.block_diag(*[pool_w[l, g] for g in range(POOL_GROUPS)]).astype(BF16)
        q, k, v, o, gates, qsb, ksb, vsb, yp = _inproj(
            xf, row(pre_mix_g[l]), wm, wg, mlstm_conv_w[l], row(mlstm_conv_b[l]), gb, pw,
            row(pool_scale[l]), row(pool_out_g[l]), seq=seq)
        ym = _mlstm(q, k, v, o, gates, row(mlstm_out_g[l]), batch=batch, seq=seq)
        ysb = _sb(qsb, ksb, vsb, batch=batch, seq=seq)
        fc = FFN_FC
        up_w, cw, cb = ffn_w_up[l], ffn_conv_w[l], ffn_conv_b[l].reshape(1, -1)
        xf = _outffn(
            xf, ym, ysb, yp, row(sb_out_g[l]), w_out[l].astype(BF16), row(post_mix_g[l]), row(pre_ffn_g[l]),
            _chunked_cols(up_w[:, :D_FF], fc).astype(BF16), _chunked_cols(up_w[:, D_FF:], fc).astype(BF16),
            _chunked_cols(cw[:, :D_FF], fc), _chunked_cols(cw[:, D_FF:], fc),
            _chunked_cols(cb[:, :D_FF], fc), _chunked_cols(cb[:, D_FF:], fc),
            ffn_w_down[l].reshape(D_FF // fc, fc, D_MODEL).astype(BF16), row(post_ffn_g[l]), seq=seq)
    return xf.reshape(batch, seq, d_model)
```

```python
import functools

import numpy as np
import jax
import jax.numpy as jnp
from jax import lax
from jax.experimental import pallas as pl
from jax.experimental.pallas import tpu as pltpu

F32 = jnp.float32
BF16 = jnp.bfloat16

D_MODEL = 1024
MLSTM_HEADS = 4
MLSTM_HEAD_DIM = 128
SB_HEADS = 4
SB_HEAD_DIM = 64
POOL_GROUPS = 4
POOL_CH = 64
POOL_WINDOWS = (2, 4, 8, 16)
W_M = MLSTM_HEADS * MLSTM_HEAD_DIM
W_SB = SB_HEADS * SB_HEAD_DIM
W_P = POOL_GROUPS * POOL_CH
MLSTM_CONV = 4
FFN_CONV = 3
D_FF = 2816
EPS = 1e-6
LOG2E = 1.4426950408889634

LANES = 128
SUBLANES = 8
GATE_W = LANES
QK_HALO = SUBLANES
POOL_HALO = 2 * SUBLANES
FFN_HALO = SUBLANES

TM_IN = 512
TM_FFN = 512
MLSTM_L = 256
SB_TQ = 256
SB_TK = 256
SB_DEAD = 160.0
FFN_FC = 256
VMEM_LIMIT = 56 * 1024 * 1024


def _rms(x, g):
    return x * lax.rsqrt(jnp.mean(x * x, axis=-1, keepdims=True) + EPS) * g


def _sigmoid(x):
    return 1.0 / (1.0 + jnp.exp(-x))


def _log_sigmoid(x):
    return jnp.minimum(x, 0.0) - jnp.log1p(jnp.exp(-jnp.abs(x)))


def _const_spec(shape):
    nd = len(shape)
    return pl.BlockSpec(shape, lambda *_: (0,) * nd, pipeline_mode=pl.Buffered(1))


def _inproj_kernel(x_ref, g_ref, wm_ref, wg_ref, cw_ref, cb_ref, gb_ref, pw_ref, ps_ref, pg_ref,
                   q_ref, k_ref, v_ref, o_ref, gate_ref, qsb_ref, ksb_ref, vsb_ref, yp_ref,
                   qk_buf, u_buf, *, tiles_per_seq):
    tm = x_ref.shape[0]
    tile = pl.program_id(0) % tiles_per_seq
    h = _rms(x_ref[...], g_ref[...]).astype(BF16)

    @pl.when(tile == 0)
    def _():
        qk_buf[0:QK_HALO, :] = jnp.zeros((QK_HALO, 2 * W_M), F32)
        u_buf[0:POOL_HALO, :] = jnp.zeros((POOL_HALO, W_P), F32)

    @pl.when(tile != 0)
    def _():
        qk_buf[0:QK_HALO, :] = qk_buf[tm:tm + QK_HALO, :]
        u_buf[0:POOL_HALO, :] = u_buf[tm:tm + POOL_HALO, :]

    c0 = 0
    qk_buf[QK_HALO:QK_HALO + tm, :] = jnp.dot(h, wm_ref[:, c0:c0 + 2 * W_M], preferred_element_type=F32)
    c0 += 2 * W_M
    v_ref[...] = jnp.dot(h, wm_ref[:, c0:c0 + W_M], preferred_element_type=F32).astype(BF16)
    c0 += W_M
    o_ref[...] = jnp.dot(h, wm_ref[:, c0:c0 + W_M], preferred_element_type=F32).astype(BF16)
    c0 += W_M
    gates = jnp.dot(h, wg_ref[...], preferred_element_type=F32) + gb_ref[...]
    sb = jnp.dot(h, wm_ref[:, c0:c0 + 3 * W_SB], preferred_element_type=F32)
    c0 += 3 * W_SB

    acc = cb_ref[...]
    for j in range(MLSTM_CONV):
        off = QK_HALO - (MLSTM_CONV - 1) + j
        acc = acc + cw_ref[j:j + 1, :] * qk_buf[off:off + tm, :]
    qk = acc * _sigmoid(acc)
    q_ref[...] = qk[:, :W_M].astype(BF16)
    k_ref[...] = (qk[:, W_M:] * (MLSTM_HEAD_DIM ** -0.5)).astype(BF16)

    lane = lax.broadcasted_iota(jnp.int32, gates.shape, 1)
    gate_ref[...] = jnp.where(lane < MLSTM_HEADS, gates, _log_sigmoid(gates))

    for hh in range(SB_HEADS):
        lo = hh * SB_HEAD_DIM
        qsb_ref[hh] = (sb[:, lo:lo + SB_HEAD_DIM] * (SB_HEAD_DIM ** -0.5 * LOG2E)).astype(BF16)
        ksb_ref[hh] = sb[:, W_SB + lo:W_SB + lo + SB_HEAD_DIM].astype(BF16)
        vsb_ref[hh] = sb[:, 2 * W_SB + lo:2 * W_SB + lo + SB_HEAD_DIM].astype(BF16)

    u = jnp.dot(h, wm_ref[:, c0:c0 + W_P], preferred_element_type=F32)
    u_buf[POOL_HALO:POOL_HALO + tm, :] = u
    ext = u_buf[...]
    sums = []
    s = ext
    for shift in (1, 2, 4, 8):
        s = s + pltpu.roll(s, shift, 0)
        sums.append(s[POOL_HALO:, :])
    grp = lax.broadcasted_iota(jnp.int32, (tm, W_P), 1) // POOL_CH
    pos = (tile * tm + lax.broadcasted_iota(jnp.int32, (tm, W_P), 0)).astype(F32)
    win = sums[-1]
    width = jnp.full((tm, W_P), float(POOL_WINDOWS[-1]), F32)
    for gi in range(POOL_GROUPS - 2, -1, -1):
        win = jnp.where(grp == gi, sums[gi], win)
        width = jnp.where(grp == gi, float(POOL_WINDOWS[gi]), width)
    y = win / jnp.minimum(pos + 1.0, width) - u
    yp = jnp.dot(y.astype(BF16), pw_ref[...], preferred_element_type=F32) * ps_ref[...]
    yp_ref[...] = _rms(yp, pg_ref[...]).astype(BF16)


def _inproj(x, g, wm, wg, cw, cb, gb, pw, ps, pg, *, seq):
    n = x.shape[0]
    tm = TM_IN
    tok = lambda w: pl.BlockSpec((tm, w), lambda i: (i, 0))
    head = pl.BlockSpec((SB_HEADS, tm, SB_HEAD_DIM), lambda i: (0, i, 0))
    sds = jax.ShapeDtypeStruct
    return pl.pallas_call(
        functools.partial(_inproj_kernel, tiles_per_seq=seq // tm),
        grid=(n // tm,),
        in_specs=[tok(D_MODEL), _const_spec(g.shape), _const_spec(wm.shape), _const_spec(wg.shape),
                  _const_spec(cw.shape), _const_spec(cb.shape), _const_spec(gb.shape), _const_spec(pw.shape),
                  _const_spec(ps.shape), _const_spec(pg.shape)],
        out_specs=[tok(W_M), tok(W_M), tok(W_M), tok(W_M), tok(GATE_W), head, head, head, tok(W_P)],
        out_shape=[sds((n, W_M), BF16), sds((n, W_M), BF16), sds((n, W_M), BF16), sds((n, W_M), BF16),
                   sds((n, GATE_W), F32),
                   sds((SB_HEADS, n, SB_HEAD_DIM), BF16), sds((SB_HEADS, n, SB_HEAD_DIM), BF16),
                   sds((SB_HEADS, n, SB_HEAD_DIM), BF16), sds((n, W_P), BF16)],
        scratch_shapes=[pltpu.VMEM((tm + QK_HALO, 2 * W_M), F32), pltpu.VMEM((tm + POOL_HALO, W_P), F32)],
        compiler_params=pltpu.CompilerParams(dimension_semantics=("arbitrary",), vmem_limit_bytes=VMEM_LIMIT),
        name="inproj",
    )(x, g, wm, wg, cw, cb, gb, pw, ps, pg)


def _split3(x):
    hi = x.astype(BF16)
    r = x - hi.astype(F32)
    mid = r.astype(BF16)
    return hi, mid, (r - mid.astype(F32)).astype(BF16)


def _mlstm_kernel(q_ref, k_ref, v_ref, o_ref, gc_ref, gr_ref, og_ref, tl_ref, tu_ref, y_ref, c_ref, m_ref):
    L = q_ref.shape[0]
    dh = MLSTM_HEAD_DIM
    heads = range(MLSTM_HEADS)

    @pl.when(pl.program_id(1) == 0)
    def _():
        c_ref[...] = jnp.zeros(c_ref.shape, F32)
        m_ref[...] = jnp.zeros(m_ref.shape, F32)

    gc = gc_ref[...]
    gr = gr_ref[0]
    causal = lax.broadcasted_iota(jnp.int32, (L, L), 1) <= lax.broadcasted_iota(jnp.int32, (L, L), 0)
    b_cols = jnp.dot(tl_ref[...], jnp.concatenate(_split3(gc), axis=0), preferred_element_type=F32)
    b_rows = jnp.dot(jnp.concatenate(_split3(gr), axis=1), tu_ref[...], preferred_element_type=F32)
    ones = jnp.ones((L, dh), BF16)
    hs = [slice(h * dh, (h + 1) * dh) for h in heads]

    qk, qc, v_aug, c_aug = [], [], [], []
    for h in heads:
        q = q_ref[:, hs[h]]
        v_aug.append(jnp.concatenate([v_ref[:, hs[h]], ones], axis=1))
        c_aug.append(c_ref[h])
        qk.append(lax.dot_general(q, k_ref[:, hs[h]], (((1,), (1,)), ((), ())), preferred_element_type=F32))
        qc.append(jnp.dot(q, c_aug[h].astype(BF16), preferred_element_type=F32))

    hv, cols = [], []
    for h in heads:
        f = MLSTM_HEADS + h
        b_c = b_cols[:, f:f + 1]
        i_c = gc[:, h:h + 1]
        w_r = gr[h:h + 1, :] - b_rows[f:f + 1, :]
        m_prev = m_ref[h:h + 1, 0:1]
        cm = jnp.max(jnp.where(causal, w_r, -jnp.inf), axis=1, keepdims=True)
        inter = b_c + m_prev
        m_t = jnp.maximum(inter, b_c + cm)
        dexp = jnp.exp2(jnp.where(causal, (b_c - m_t) * LOG2E + w_r * LOG2E, -jnp.inf))
        hv.append(jnp.dot((qk[h] * dexp).astype(BF16), v_aug[h], preferred_element_type=F32))
        cols.append((b_c, i_c, cm, inter, m_t, m_prev))

    ys = []
    for h in heads:
        b_c, i_c, cm, inter, m_t, m_prev = cols[h]
        tot = hv[h] + jnp.exp(inter - m_t) * qc[h]
        hh = tot[:, :dh] / jnp.maximum(jnp.abs(tot[:, dh:]), jnp.exp(-m_t))
        ys.append(_sigmoid(o_ref[:, hs[h]].astype(F32)) * hh)
        b_end = b_c[L - 1:L, :]
        m_new = jnp.maximum(b_end + m_prev, b_end + cm[L - 1:L, :])
        decay = jnp.exp(b_end + m_prev - m_new)
        kw = (k_ref[:, hs[h]].astype(F32) * jnp.exp(b_end - b_c + i_c - m_new)).astype(BF16)
        c_ref[h] = decay * c_aug[h] + lax.dot_general(kw, v_aug[h], (((0,), (0,)), ((), ())),
                                                      preferred_element_type=F32)
        m_ref[h:h + 1, :] = jnp.broadcast_to(m_new, (1, LANES))
    y_ref[...] = _rms(jnp.concatenate(ys, axis=1), og_ref[...]).astype(BF16)


def _mlstm(q, k, v, o, gates, og, *, batch, seq):
    n = q.shape[0]
    L = MLSTM_L
    nc = seq // L
    gr = jnp.swapaxes(gates[:, :2 * MLSTM_HEADS].reshape(n // L, L, 2 * MLSTM_HEADS), 1, 2)
    lower = (np.arange(L)[None, :] <= np.arange(L)[:, None]).astype(np.float32)
    tl = jnp.asarray(np.concatenate([lower] * 3, axis=1), BF16)
    tu = jnp.asarray(np.concatenate([lower.T] * 3, axis=0), BF16)
    tok = lambda w: pl.BlockSpec((L, w), lambda b, t: (b * nc + t, 0))
    return pl.pallas_call(
        _mlstm_kernel,
        grid=(batch, nc),
        in_specs=[tok(W_M), tok(W_M), tok(W_M), tok(W_M), tok(GATE_W),
                  pl.BlockSpec((1, 2 * MLSTM_HEADS, L), lambda b, t: (b * nc + t, 0, 0)),
                  pl.BlockSpec(og.shape, lambda b, t: (0, 0)),
                  pl.BlockSpec(tl.shape, lambda b, t: (0, 0)), pl.BlockSpec(tu.shape, lambda b, t: (0, 0))],
        out_specs=tok(W_M),
        out_shape=jax.ShapeDtypeStruct((n, W_M), BF16),
        scratch_shapes=[pltpu.VMEM((MLSTM_HEADS, MLSTM_HEAD_DIM, 2 * MLSTM_HEAD_DIM), F32),
                        pltpu.VMEM((SUBLANES, LANES), F32)],
        compiler_params=pltpu.CompilerParams(dimension_semantics=("arbitrary", "arbitrary"),
                                             vmem_limit_bytes=VMEM_LIMIT),
        name="mlstm",
    )(q, k, v, o, gates, gr, og, tl, tu)


def _sb_kernel(q_ref, k_ref, v_ref, mm_ref, o_ref):
    nh, tq, _ = q_ref.shape
    tk = SB_TK
    i = pl.program_id(1)
    row = lax.broadcasted_iota(jnp.int32, (tq, tk), 0)
    col = lax.broadcasted_iota(jnp.int32, (tq, tk), 1)
    valid = col < row

    def step(j, state, masked):
        start = pl.multiple_of(j * tk, tk)
        zs, lbs, firsts, css, out = [], [], [], [], []
        for h in range(nh):
            kb = k_ref[h, pl.ds(start, tk), :]
            zs.append(lax.dot_general(q_ref[h], kb, (((1,), (1,)), ((), ())), preferred_element_type=F32))
        for h in range(nh):
            z = zs[h]
            nl = jnp.maximum(z, 0.0) + jnp.log(1.0 + jnp.exp2(-jnp.abs(z))) * LOG2E
            lbs.append(z - nl)
            if masked:
                nl = jnp.where(valid, nl, 0.0)
            hi = nl.astype(BF16)
            lo = (nl - hi.astype(F32)).astype(BF16)
            firsts.append(nl[:, 0:1])
            css.append(jnp.dot(jnp.concatenate([hi, lo], axis=1), mm_ref[...], preferred_element_type=F32))
        for h in range(nh):
            carry, acc = state[h]
            after = css[h] + jnp.concatenate([carry] * (tk // LANES), axis=1)
            a = jnp.exp2(lbs[h] - after)
            if masked:
                a = jnp.where(valid, a, 0.0)
            vb = v_ref[h, pl.ds(start, tk), :]
            acc = acc + jnp.dot(a.astype(BF16), vb, preferred_element_type=F32)
            total = css[h][:, 0:1] + firsts[h]
            out.append((carry + jnp.broadcast_to(total, (tq, LANES)), acc))
        return tuple(out)

    def least_carry(st):
        m = st[0][0]
        for h in range(1, nh):
            m = jnp.minimum(m, st[h][0])
        return jnp.min(m)

    def cond(c):
        return jnp.logical_and(c[0] >= 0, c[1] < SB_DEAD)

    def body(c):
        st = step(c[0], c[2], False)
        return c[0] - 1, least_carry(st), st

    state = tuple((jnp.zeros((tq, LANES), F32), jnp.zeros((tq, SB_HEAD_DIM), F32)) for _ in range(nh))
    state = step(i, state, True)
    state = lax.while_loop(cond, body, (i - 1, least_carry(state), state))[2]
    for h in range(nh):
        o_ref[h] = state[h][1].astype(BF16)


def _sb_matrix():
    tk = SB_TK
    later = (np.arange(tk)[:, None] > np.arange(tk)[None, :]).astype(np.float32)
    return jnp.asarray(np.concatenate([later, later], axis=0), BF16)


def _sb(q, k, v, *, batch, seq):
    n = q.shape[1]
    assert SB_TQ == SB_TK
    nq = seq // SB_TQ
    mm = _sb_matrix()
    return pl.pallas_call(
        _sb_kernel,
        grid=(batch, nq),
        in_specs=[pl.BlockSpec((SB_HEADS, SB_TQ, SB_HEAD_DIM), lambda b, i: (0, b * nq + i, 0)),
                  pl.BlockSpec((SB_HEADS, seq, SB_HEAD_DIM), lambda b, i: (0, b, 0)),
                  pl.BlockSpec((SB_HEADS, seq, SB_HEAD_DIM), lambda b, i: (0, b, 0)),
                  pl.BlockSpec(mm.shape, lambda b, i: (0, 0))],
        out_specs=pl.BlockSpec((SB_HEADS, SB_TQ, SB_HEAD_DIM), lambda b, i: (0, b * nq + i, 0)),
        out_shape=jax.ShapeDtypeStruct((SB_HEADS, n, SB_HEAD_DIM), BF16),
        compiler_params=pltpu.CompilerParams(dimension_semantics=("arbitrary", "arbitrary"),
                                             vmem_limit_bytes=VMEM_LIMIT),
        name="stickbreak",
    )(q, k, v, mm)


def _outffn_kernel(x_ref, ym_ref, ysb_ref, yp_ref, sbg_ref, wo_ref, pmg_ref, pfg_ref,
                   wu_ref, cw_ref, cb_ref, wd_ref, pog_ref,
                   out_ref, h_buf, act_buf, up_buf, halo_buf, *, tiles_per_seq):
    tm = x_ref.shape[0]
    fc = FFN_FC
    nchunk = D_FF // fc
    tile = pl.program_id(0) % tiles_per_seq

    ysb = jnp.concatenate([ysb_ref[hh].astype(F32) for hh in range(SB_HEADS)], axis=1)
    mix = jnp.concatenate([ym_ref[...], _rms(ysb, sbg_ref[...]).astype(BF16), yp_ref[...]], axis=1)
    x1 = x_ref[...] + _rms(jnp.dot(mix, wo_ref[...], preferred_element_type=F32), pmg_ref[...])
    h_buf[...] = _rms(x1, pfg_ref[...]).astype(BF16)

    @pl.when(tile == 0)
    def _():
        halo_buf[...] = jnp.zeros(halo_buf.shape, F32)

    def cols(c, half):
        lo = half * D_FF + c * fc
        return slice(lo, lo + fc)

    def up_dots(c):
        hb = h_buf[...]
        return tuple(jnp.dot(hb, wu_ref[:, cols(c, half)], preferred_element_type=F32) for half in (0, 1))

    def conv_half(c, half, up):
        buf = up_buf.at[2 * (c % 2) + half]
        buf[0:FFN_HALO, :] = halo_buf[2 * c + half]
        buf[FFN_HALO:FFN_HALO + tm, :] = up
        halo_buf[2 * c + half] = up[tm - FFN_HALO:, :]
        acc = cb_ref[:, cols(c, half)]
        for j in range(FFN_CONV):
            off = FFN_HALO - (FFN_CONV - 1) + j
            acc = acc + cw_ref[j:j + 1, cols(c, half)] * buf[off:off + tm, :]
        return acc

    ups = up_dots(0)
    for c in range(nchunk):
        nxt = up_dots(c + 1) if c + 1 < nchunk else None
        gate = conv_half(c, 0, ups[0])
        val = conv_half(c, 1, ups[1])
        act_buf[c] = (jax.nn.gelu(gate, approximate=True) * val).astype(BF16)
        ups = nxt
    ffn = jnp.dot(act_buf[0], wd_ref[0:fc, :], preferred_element_type=F32)
    for c in range(1, nchunk):
        ffn = ffn + jnp.dot(act_buf[c], wd_ref[c * fc:(c + 1) * fc, :], preferred_element_type=F32)
    out_ref[...] = x1 + _rms(ffn, pog_ref[...])


def _outffn(x, ym, ysb, yp, sbg, wo, pmg, pfg, wu, cw, cb, wd, pog, *, seq):
    n = x.shape[0]
    tm = TM_FFN
    fc = FFN_FC
    nchunk = D_FF // fc
    tok = lambda w: pl.BlockSpec((tm, w), lambda i: (i, 0))
    consts = (sbg, wo, pmg, pfg, wu, cw, cb, wd, pog)
    return pl.pallas_call(
        functools.partial(_outffn_kernel, tiles_per_seq=seq // tm),
        grid=(n // tm,),
        in_specs=[tok(D_MODEL), tok(W_M), pl.BlockSpec((SB_HEADS, tm, SB_HEAD_DIM), lambda i: (0, i, 0)), tok(W_P)]
                 + [_const_spec(c.shape) for c in consts],
        out_specs=tok(D_MODEL),
        out_shape=jax.ShapeDtypeStruct((n, D_MODEL), F32),
        scratch_shapes=[pltpu.VMEM((tm, D_MODEL), BF16), pltpu.VMEM((nchunk, tm, fc), BF16),
                        pltpu.VMEM((4, tm + FFN_HALO, fc), F32), pltpu.VMEM((2 * nchunk, FFN_HALO, fc), F32)],
        compiler_params=pltpu.CompilerParams(dimension_semantics=("arbitrary",), vmem_limit_bytes=VMEM_LIMIT),
        name="outffn",
    )(x, ym, ysb, yp, *consts)


def kernel(x, pre_mix_g, w_in, mlstm_conv_w, mlstm_conv_b, i_bias, f_bias, pool_w, pool_scale, mlstm_out_g,
           sb_out_g, pool_out_g, w_out, post_mix_g, pre_ffn_g, ffn_w_up, ffn_conv_w, ffn_conv_b, ffn_w_down,
           post_ffn_g):
    batch, seq, d_model = x.shape
    depth = w_in.shape[0]
    assert d_model == D_MODEL and ffn_w_down.shape[1] == D_FF
    assert seq % TM_IN == 0 and seq % TM_FFN == 0 and seq % MLSTM_L == 0 and seq % SB_TQ == 0
    n = batch * seq
    xf = x.reshape(n, d_model)
    row = lambda a: a.reshape(1, -1).astype(F32)
    o_gate = 4 * W_M
    o_sb = o_gate + 2 * MLSTM_HEADS
    for l in range(depth):
        w = w_in[l]
        wm = jnp.concatenate([w[:, :o_gate], w[:, o_sb:]], axis=1).astype(BF16)
        wg = jnp.pad(w[:, o_gate:o_sb], ((0, 0), (0, GATE_W - 2 * MLSTM_HEADS))).astype(BF16)
        gb = jnp.pad(jnp.concatenate([i_bias[l], f_bias[l]]), (0, GATE_W - 2 * MLSTM_HEADS)).reshape(1, GATE_W)
        pw = jax.scipy.linalg.block_diag(*[pool_w[l, g] for g in range(POOL_GROUPS)]).astype(BF16)
        q, k, v, o, gates, qsb, ksb, vsb, yp = _inproj(
            xf, row(pre_mix_g[l]), wm, wg, mlstm_conv_w[l], row(mlstm_conv_b[l]), gb, pw,
            row(pool_scale[l]), row(pool_out_g[l]), seq=seq)
        ym = _mlstm(q, k, v, o, gates, row(mlstm_out_g[l]), batch=batch, seq=seq)
        ysb = _sb(qsb, ksb, vsb, batch=batch, seq=seq)
        xf = _outffn(
            xf, ym, ysb, yp, row(sb_out_g[l]), w_out[l].astype(BF16), row(post_mix_g[l]), row(pre_ffn_g[l]),
            ffn_w_up[l].astype(BF16), ffn_conv_w[l], row(ffn_conv_b[l]), ffn_w_down[l].astype(BF16),
            row(post_ffn_g[l]), seq=seq)
    return xf.reshape(batch, seq, d_model)
```

```python
import functools

import numpy as np
import jax
import jax.numpy as jnp
from jax import lax
from jax.experimental import pallas as pl
from jax.experimental.pallas import tpu as pltpu

F32 = jnp.float32
BF16 = jnp.bfloat16

D_MODEL = 1024
MLSTM_HEADS = 4
MLSTM_HEAD_DIM = 128
SB_HEADS = 4
SB_HEAD_DIM = 64
POOL_GROUPS = 4
POOL_CH = 64
POOL_WINDOWS = (2, 4, 8, 16)
W_M = MLSTM_HEADS * MLSTM_HEAD_DIM
W_SB = SB_HEADS * SB_HEAD_DIM
W_P = POOL_GROUPS * POOL_CH
MLSTM_CONV = 4
FFN_CONV = 3
D_FF = 2816
EPS = 1e-6
LOG2E = 1.4426950408889634

LANES = 128
SUBLANES = 8
GATE_W = LANES
QK_HALO = SUBLANES
POOL_HALO = 2 * SUBLANES
FFN_HALO = SUBLANES

TM_IN = 512
TM_FFN = 512
MLSTM_L = 256
SB_TQ = 256
SB_TK = 256
SB_DEAD = 160.0
FFN_FC = 256
VMEM_LIMIT = 56 * 1024 * 1024


def _rms(x, g):
    return x * lax.rsqrt(jnp.mean(x * x, axis=-1, keepdims=True) + EPS) * g


def _sigmoid(x):
    return 1.0 / (1.0 + jnp.exp(-x))


def _log_sigmoid(x):
    return jnp.minimum(x, 0.0) - jnp.log1p(jnp.exp(-jnp.abs(x)))


def _causal_conv(ext, w, b, halo):
    taps = w.shape[0]
    acc = b + w[taps - 1:taps, :] * ext[halo:, :]
    for back in range(1, taps):
        acc = acc + w[taps - 1 - back:taps - back, :] * pltpu.roll(ext, back, 0)[halo:, :]
    return acc


def _gelu_tanh(x):
    k = -2.0 * 0.7978845608028654 * LOG2E
    return x / (1.0 + jnp.exp2(x * (x * x * (k * 0.044715) + k)))


def _const_spec(shape):
    nd = len(shape)
    return pl.BlockSpec(shape, lambda *_: (0,) * nd, pipeline_mode=pl.Buffered(1))


def _layer_spec(arr, layer):
    return pl.BlockSpec((None,) + arr.shape[1:], lambda *_: (layer, 0, 0), pipeline_mode=pl.Buffered(1))


def _inproj_kernel(x_ref, g_ref, wa_ref, wb_ref, wg_ref, cw_ref, cb_ref, gb_ref, pw_ref, ps_ref, pg_ref,
                   q_ref, k_ref, v_ref, o_ref, gate_ref, qsb_ref, ksb_ref, vsb_ref, yp_ref,
                   qk_buf, u_buf, *, tiles_per_seq):
    tm = x_ref.shape[0]
    tile = pl.program_id(0) % tiles_per_seq
    h = _rms(x_ref[...], g_ref[...]).astype(BF16)

    @pl.when(tile == 0)
    def _():
        qk_buf[0:QK_HALO, :] = jnp.zeros((QK_HALO, 2 * W_M), F32)
        u_buf[0:POOL_HALO, :] = jnp.zeros((POOL_HALO, W_P), F32)

    @pl.when(tile != 0)
    def _():
        qk_buf[0:QK_HALO, :] = qk_buf[tm:tm + QK_HALO, :]
        u_buf[0:POOL_HALO, :] = u_buf[tm:tm + POOL_HALO, :]

    qk_buf[QK_HALO:QK_HALO + tm, :] = jnp.dot(h, wa_ref[:, 0:2 * W_M], preferred_element_type=F32)
    v_ref[...] = jnp.dot(h, wa_ref[:, 2 * W_M:3 * W_M], preferred_element_type=F32).astype(BF16)
    o_ref[...] = jnp.dot(h, wa_ref[:, 3 * W_M:4 * W_M], preferred_element_type=F32).astype(BF16)
    gates = jnp.dot(h, wg_ref[...], preferred_element_type=F32) + gb_ref[...]
    sb = jnp.dot(h, wb_ref[:, 0:3 * W_SB], preferred_element_type=F32)

    acc = _causal_conv(qk_buf[...], cw_ref[...], cb_ref[...], QK_HALO)
    qk = acc * _sigmoid(acc)
    q_ref[...] = qk[:, :W_M].astype(BF16)
    k_ref[...] = (qk[:, W_M:] * (MLSTM_HEAD_DIM ** -0.5)).astype(BF16)

    lane = lax.broadcasted_iota(jnp.int32, gates.shape, 1)
    gate_ref[...] = jnp.where(lane < MLSTM_HEADS, gates, _log_sigmoid(gates))

    for hh in range(SB_HEADS):
        lo = hh * SB_HEAD_DIM
        qsb_ref[hh] = (sb[:, lo:lo + SB_HEAD_DIM] * (SB_HEAD_DIM ** -0.5 * LOG2E)).astype(BF16)
        ksb_ref[hh] = sb[:, W_SB + lo:W_SB + lo + SB_HEAD_DIM].astype(BF16)
        vsb_ref[hh] = sb[:, 2 * W_SB + lo:2 * W_SB + lo + SB_HEAD_DIM].astype(BF16)

    u = jnp.dot(h, wb_ref[:, 3 * W_SB:3 * W_SB + W_P], preferred_element_type=F32)
    u_buf[POOL_HALO:POOL_HALO + tm, :] = u
    ext = u_buf[...]
    sums = []
    s = ext
    for shift in (1, 2, 4, 8):
        s = s + pltpu.roll(s, shift, 0)
        sums.append(s[POOL_HALO:, :])
    grp = lax.broadcasted_iota(jnp.int32, (tm, W_P), 1) // POOL_CH
    pos = (tile * tm + lax.broadcasted_iota(jnp.int32, (tm, W_P), 0)).astype(F32)
    win = sums[-1]
    width = jnp.full((tm, W_P), float(POOL_WINDOWS[-1]), F32)
    for gi in range(POOL_GROUPS - 2, -1, -1):
        win = jnp.where(grp == gi, sums[gi], win)
        width = jnp.where(grp == gi, float(POOL_WINDOWS[gi]), width)
    y = win / jnp.minimum(pos + 1.0, width) - u
    yp = jnp.dot(y.astype(BF16), pw_ref[...], preferred_element_type=F32) * ps_ref[...]
    yp_ref[...] = _rms(yp, pg_ref[...]).astype(BF16)


def _inproj(x, *params, layer, seq):
    n = x.shape[0]
    tm = TM_IN
    tok = lambda w: pl.BlockSpec((tm, w), lambda i: (i, 0))
    head = pl.BlockSpec((SB_HEADS, tm, SB_HEAD_DIM), lambda i: (0, i, 0))
    sds = jax.ShapeDtypeStruct
    return pl.pallas_call(
        functools.partial(_inproj_kernel, tiles_per_seq=seq // tm),
        grid=(n // tm,),
        in_specs=[tok(D_MODEL)] + [_layer_spec(p, layer) for p in params],
        out_specs=[tok(W_M), tok(W_M), tok(W_M), tok(W_M), tok(GATE_W), head, head, head, tok(W_P)],
        out_shape=[sds((n, W_M), BF16), sds((n, W_M), BF16), sds((n, W_M), BF16), sds((n, W_M), BF16),
                   sds((n, GATE_W), F32),
                   sds((SB_HEADS, n, SB_HEAD_DIM), BF16), sds((SB_HEADS, n, SB_HEAD_DIM), BF16),
                   sds((SB_HEADS, n, SB_HEAD_DIM), BF16), sds((n, W_P), BF16)],
        scratch_shapes=[pltpu.VMEM((tm + QK_HALO, 2 * W_M), F32), pltpu.VMEM((tm + POOL_HALO, W_P), F32)],
        compiler_params=pltpu.CompilerParams(dimension_semantics=("arbitrary",), vmem_limit_bytes=VMEM_LIMIT),
        name="inproj",
    )(x, *params)


def _split3(x):
    hi = x.astype(BF16)
    r = x - hi.astype(F32)
    mid = r.astype(BF16)
    return hi, mid, (r - mid.astype(F32)).astype(BF16)


def _mlstm_kernel(q_ref, k_ref, v_ref, o_ref, gc_ref, gr_ref, og_ref, tl_ref, tu_ref, y_ref, c_ref, m_ref):
    L = q_ref.shape[0]
    dh = MLSTM_HEAD_DIM
    heads = range(MLSTM_HEADS)

    @pl.when(pl.program_id(1) == 0)
    def _():
        c_ref[...] = jnp.zeros(c_ref.shape, F32)
        m_ref[...] = jnp.zeros(m_ref.shape, F32)

    gc = gc_ref[...]
    gr = gr_ref[0]
    causal = lax.broadcasted_iota(jnp.int32, (L, L), 1) <= lax.broadcasted_iota(jnp.int32, (L, L), 0)
    b_cols = jnp.dot(tl_ref[...], jnp.concatenate(_split3(gc), axis=0), preferred_element_type=F32)
    b_rows = jnp.dot(jnp.concatenate(_split3(gr), axis=1), tu_ref[...], preferred_element_type=F32)
    ones = jnp.ones((L, dh), BF16)
    hs = [slice(h * dh, (h + 1) * dh) for h in heads]

    qk, qc, v_aug, c_aug = [], [], [], []
    for h in heads:
        q = q_ref[:, hs[h]]
        v_aug.append(jnp.concatenate([v_ref[:, hs[h]], ones], axis=1))
        c_aug.append(c_ref[h])
        qk.append(lax.dot_general(q, k_ref[:, hs[h]], (((1,), (1,)), ((), ())), preferred_element_type=F32))
        qc.append(jnp.dot(q, c_aug[h].astype(BF16), preferred_element_type=F32))

    hv, cols = [], []
    for h in heads:
        f = MLSTM_HEADS + h
        b_c = b_cols[:, f:f + 1]
        i_c = gc[:, h:h + 1]
        w_r = gr[h:h + 1, :] - b_rows[f:f + 1, :]
        m_prev = m_ref[h:h + 1, 0:1]
        cm = jnp.max(jnp.where(causal, w_r, -jnp.inf), axis=1, keepdims=True)
        inter = b_c + m_prev
        m_t = jnp.maximum(inter, b_c + cm)
        dexp = jnp.exp2(jnp.where(causal, (b_c - m_t) * LOG2E + w_r * LOG2E, -jnp.inf))
        hv.append(jnp.dot((qk[h] * dexp).astype(BF16), v_aug[h], preferred_element_type=F32))
        cols.append((b_c, i_c, cm, inter, m_t, m_prev))

    ys = []
    for h in heads:
        b_c, i_c, cm, inter, m_t, m_prev = cols[h]
        tot = hv[h] + jnp.exp(inter - m_t) * qc[h]
        hh = tot[:, :dh] / jnp.maximum(jnp.abs(tot[:, dh:]), jnp.exp(-m_t))
        ys.append(_sigmoid(o_ref[:, hs[h]].astype(F32)) * hh)
        b_end = b_c[L - 1:L, :]
        m_new = jnp.maximum(b_end + m_prev, b_end + cm[L - 1:L, :])
        decay = jnp.exp(b_end + m_prev - m_new)
        kw = (k_ref[:, hs[h]].astype(F32) * jnp.exp(b_end - b_c + i_c - m_new)).astype(BF16)
        c_ref[h] = decay * c_aug[h] + lax.dot_general(kw, v_aug[h], (((0,), (0,)), ((), ())),
                                                      preferred_element_type=F32)
        m_ref[h:h + 1, :] = jnp.broadcast_to(m_new, (1, LANES))
    y_ref[...] = _rms(jnp.concatenate(ys, axis=1), og_ref[...]).astype(BF16)


def _mlstm(q, k, v, o, gates, og, *, layer, batch, seq):
    n = q.shape[0]
    L = MLSTM_L
    nc = seq // L
    gr = jnp.swapaxes(gates[:, :2 * MLSTM_HEADS].reshape(n // L, L, 2 * MLSTM_HEADS), 1, 2)
    lower = (np.arange(L)[None, :] <= np.arange(L)[:, None]).astype(np.float32)
    tl = jnp.asarray(np.concatenate([lower] * 3, axis=1), BF16)
    tu = jnp.asarray(np.concatenate([lower.T] * 3, axis=0), BF16)
    tok = lambda w: pl.BlockSpec((L, w), lambda b, t: (b * nc + t, 0))
    return pl.pallas_call(
        _mlstm_kernel,
        grid=(batch, nc),
        in_specs=[tok(W_M), tok(W_M), tok(W_M), tok(W_M), tok(GATE_W),
                  pl.BlockSpec((1, 2 * MLSTM_HEADS, L), lambda b, t: (b * nc + t, 0, 0)),
                  _layer_spec(og, layer), _const_spec(tl.shape), _const_spec(tu.shape)],
        out_specs=tok(W_M),
        out_shape=jax.ShapeDtypeStruct((n, W_M), BF16),
        scratch_shapes=[pltpu.VMEM((MLSTM_HEADS, MLSTM_HEAD_DIM, 2 * MLSTM_HEAD_DIM), F32),
                        pltpu.VMEM((SUBLANES, LANES), F32)],
        compiler_params=pltpu.CompilerParams(dimension_semantics=("arbitrary", "arbitrary"),
                                             vmem_limit_bytes=VMEM_LIMIT),
        name="mlstm",
    )(q, k, v, o, gates, gr, og, tl, tu)


def _sb_kernel(q_ref, k_ref, v_ref, mm_ref, o_ref):
    nh, tq, _ = q_ref.shape
    tk = SB_TK
    i = pl.program_id(1)
    row = lax.broadcasted_iota(jnp.int32, (tq, tk), 0)
    col = lax.broadcasted_iota(jnp.int32, (tq, tk), 1)
    valid = col < row

    def step(j, state, masked):
        start = pl.multiple_of(j * tk, tk)
        zs, lbs, firsts, css, out = [], [], [], [], []
        for h in range(nh):
            kb = k_ref[h, pl.ds(start, tk), :]
            zs.append(lax.dot_general(q_ref[h], kb, (((1,), (1,)), ((), ())), preferred_element_type=F32))
        for h in range(nh):
            z = zs[h]
            nl = jnp.maximum(z, 0.0) + jnp.log(1.0 + jnp.exp2(-jnp.abs(z))) * LOG2E
            lbs.append(z - nl)
            if masked:
                nl = jnp.where(valid, nl, 0.0)
            hi = nl.astype(BF16)
            lo = (nl - hi.astype(F32)).astype(BF16)
            firsts.append(nl[:, 0:1])
            css.append(jnp.dot(jnp.concatenate([hi, lo], axis=1), mm_ref[...], preferred_element_type=F32))
        for h in range(nh):
            carry, acc = state[h]
            after = css[h] + jnp.concatenate([carry] * (tk // LANES), axis=1)
            a = jnp.exp2(lbs[h] - after)
            if masked:
                a = jnp.where(valid, a, 0.0)
            vb = v_ref[h, pl.ds(start, tk), :]
            acc = acc + jnp.dot(a.astype(BF16), vb, preferred_element_type=F32)
            total = css[h][:, 0:1] + firsts[h]
            out.append((carry + jnp.broadcast_to(total, (tq, LANES)), acc))
        return tuple(out)

    def least_carry(st):
        m = st[0][0]
        for h in range(1, nh):
            m = jnp.minimum(m, st[h][0])
        return jnp.min(m)

    def cond(c):
        return jnp.logical_and(c[0] >= 0, c[1] < SB_DEAD)

    def body(c):
        st = step(c[0], c[2], False)
        return c[0] - 1, least_carry(st), st

    state = tuple((jnp.zeros((tq, LANES), F32), jnp.zeros((tq, SB_HEAD_DIM), F32)) for _ in range(nh))
    state = step(i, state, True)
    state = lax.while_loop(cond, body, (i - 1, least_carry(state), state))[2]
    for h in range(nh):
        o_ref[h] = state[h][1].astype(BF16)


def _sb_matrix():
    tk = SB_TK
    later = (np.arange(tk)[:, None] > np.arange(tk)[None, :]).astype(np.float32)
    return jnp.asarray(np.concatenate([later, later], axis=0), BF16)


def _sb(q, k, v, *, batch, seq):
    n = q.shape[1]
    assert SB_TQ == SB_TK
    nq = seq // SB_TQ
    mm = _sb_matrix()
    return pl.pallas_call(
        _sb_kernel,
        grid=(batch, nq),
        in_specs=[pl.BlockSpec((SB_HEADS, SB_TQ, SB_HEAD_DIM), lambda b, i: (0, b * nq + i, 0)),
                  pl.BlockSpec((SB_HEADS, seq, SB_HEAD_DIM), lambda b, i: (0, b, 0)),
                  pl.BlockSpec((SB_HEADS, seq, SB_HEAD_DIM), lambda b, i: (0, b, 0)),
                  pl.BlockSpec(mm.shape, lambda b, i: (0, 0))],
        out_specs=pl.BlockSpec((SB_HEADS, SB_TQ, SB_HEAD_DIM), lambda b, i: (0, b * nq + i, 0)),
        out_shape=jax.ShapeDtypeStruct((SB_HEADS, n, SB_HEAD_DIM), BF16),
        compiler_params=pltpu.CompilerParams(dimension_semantics=("arbitrary", "arbitrary"),
                                             vmem_limit_bytes=VMEM_LIMIT),
        name="stickbreak",
    )(q, k, v, mm)


def _outffn_kernel(x_ref, ym_ref, ysb_ref, yp_ref, sbg_ref, wo_ref, pmg_ref, pfg_ref,
                   wu_ref, cw_ref, cb_ref, wd_ref, pog_ref,
                   out_ref, h_buf, act_buf, up_buf, halo_buf, *, tiles_per_seq):
    tm = x_ref.shape[0]
    fc = FFN_FC
    nchunk = D_FF // fc
    tile = pl.program_id(0) % tiles_per_seq

    ysb = jnp.concatenate([ysb_ref[hh].astype(F32) for hh in range(SB_HEADS)], axis=1)
    mix = jnp.concatenate([ym_ref[...], _rms(ysb, sbg_ref[...]).astype(BF16), yp_ref[...]], axis=1)
    x1 = x_ref[...] + _rms(jnp.dot(mix, wo_ref[...], preferred_element_type=F32), pmg_ref[...])
    h_buf[...] = _rms(x1, pfg_ref[...]).astype(BF16)

    @pl.when(tile == 0)
    def _():
        halo_buf[...] = jnp.zeros(halo_buf.shape, F32)

    def cols(c, half):
        lo = half * D_FF + c * fc
        return slice(lo, lo + fc)

    def up_dots(c):
        hb = h_buf[...]
        return tuple(jnp.dot(hb, wu_ref[:, cols(c, half)], preferred_element_type=F32) for half in (0, 1))

    def conv_half(c, half, up):
        buf = up_buf.at[2 * (c % 2) + half]
        buf[0:FFN_HALO, :] = halo_buf[2 * c + half]
        buf[FFN_HALO:FFN_HALO + tm, :] = up
        halo_buf[2 * c + half] = up[tm - FFN_HALO:, :]
        return _causal_conv(buf[...], cw_ref[:, cols(c, half)], cb_ref[:, cols(c, half)], FFN_HALO)

    ups = up_dots(0)
    for c in range(nchunk):
        nxt = up_dots(c + 1) if c + 1 < nchunk else None
        gate = conv_half(c, 0, ups[0])
        val = conv_half(c, 1, ups[1])
        act_buf[c] = (_gelu_tanh(gate) * val).astype(BF16)
        ups = nxt
    ffn = jnp.dot(act_buf[0], wd_ref[0:fc, :], preferred_element_type=F32)
    for c in range(1, nchunk):
        ffn = ffn + jnp.dot(act_buf[c], wd_ref[c * fc:(c + 1) * fc, :], preferred_element_type=F32)
    out_ref[...] = x1 + _rms(ffn, pog_ref[...])


def _outffn(x, ym, ysb, yp, *params, layer, seq):
    n = x.shape[0]
    tm = TM_FFN
    fc = FFN_FC
    nchunk = D_FF // fc
    tok = lambda w: pl.BlockSpec((tm, w), lambda i: (i, 0))
    return pl.pallas_call(
        functools.partial(_outffn_kernel, tiles_per_seq=seq // tm),
        grid=(n // tm,),
        in_specs=[tok(D_MODEL), tok(W_M), pl.BlockSpec((SB_HEADS, tm, SB_HEAD_DIM), lambda i: (0, i, 0)), tok(W_P)]
                 + [_layer_spec(p, layer) for p in params],
        out_specs=tok(D_MODEL),
        out_shape=jax.ShapeDtypeStruct((n, D_MODEL), F32),
        scratch_shapes=[pltpu.VMEM((tm, D_MODEL), BF16), pltpu.VMEM((nchunk, tm, fc), BF16),
                        pltpu.VMEM((4, tm + FFN_HALO, fc), F32), pltpu.VMEM((2 * nchunk, FFN_HALO, fc), F32)],
        compiler_params=pltpu.CompilerParams(dimension_semantics=("arbitrary",), vmem_limit_bytes=VMEM_LIMIT),
        name="outffn",
    )(x, ym, ysb, yp, *params)


def kernel(x, pre_mix_g, w_in, mlstm_conv_w, mlstm_conv_b, i_bias, f_bias, pool_w, pool_scale, mlstm_out_g,
           sb_out_g, pool_out_g, w_out, post_mix_g, pre_ffn_g, ffn_w_up, ffn_conv_w, ffn_conv_b, ffn_w_down,
           post_ffn_g):
    batch, seq, d_model = x.shape
    depth = w_in.shape[0]
    assert d_model == D_MODEL and ffn_w_down.shape[1] == D_FF
    assert seq % TM_IN == 0 and seq % TM_FFN == 0 and seq % MLSTM_L == 0 and seq % SB_TQ == 0
    n = batch * seq
    xf = x.reshape(n, d_model)
    rows = lambda a: a.reshape(depth, 1, -1).astype(F32)
    o_gate = 4 * W_M
    o_sb = o_gate + 2 * MLSTM_HEADS
    pad_gate = GATE_W - 2 * MLSTM_HEADS
    wa = w_in[:, :, :o_gate].astype(BF16)
    wb = w_in[:, :, o_sb:].astype(BF16)
    wg = jnp.pad(w_in[:, :, o_gate:o_sb], ((0, 0), (0, 0), (0, pad_gate))).astype(BF16)
    gb = jnp.pad(jnp.concatenate([i_bias, f_bias], axis=1), ((0, 0), (0, pad_gate))).reshape(depth, 1, GATE_W)
    eye = jnp.eye(POOL_GROUPS, dtype=F32)
    pw = (pool_w[:, :, :, None, :] * eye[None, :, None, :, None]).reshape(depth, W_P, W_P).astype(BF16)
    in_params = (rows(pre_mix_g), wa, wb, wg, mlstm_conv_w, rows(mlstm_conv_b), gb, pw, rows(pool_scale),
                 rows(pool_out_g))
    out_params = (rows(sb_out_g), w_out.astype(BF16), rows(post_mix_g), rows(pre_ffn_g), ffn_w_up.astype(BF16),
                  ffn_conv_w, rows(ffn_conv_b), ffn_w_down.astype(BF16), rows(post_ffn_g))
    og = rows(mlstm_out_g)
    for l in range(depth):
        q, k, v, o, gates, qsb, ksb, vsb, yp = _inproj(xf, *in_params, layer=l, seq=seq)
        ym = _mlstm(q, k, v, o, gates, og, layer=l, batch=batch, seq=seq)
        ysb = _sb(qsb, ksb, vsb, batch=batch, seq=seq)
        xf = _outffn(xf, ym, ysb, yp, *out_params, layer=l, seq=seq)
    return xf.reshape(batch, seq, d_model)
```

```python
import functools

import numpy as np
import jax
import jax.numpy as jnp
from jax import lax
from jax.experimental import pallas as pl
from jax.experimental.pallas import tpu as pltpu

F32 = jnp.float32
BF16 = jnp.bfloat16

D_MODEL = 1024
MLSTM_HEADS = 4
MLSTM_HEAD_DIM = 128
SB_HEADS = 4
SB_HEAD_DIM = 64
POOL_GROUPS = 4
POOL_CH = 64
POOL_WINDOWS = (2, 4, 8, 16)
W_M = MLSTM_HEADS * MLSTM_HEAD_DIM
W_SB = SB_HEADS * SB_HEAD_DIM
W_P = POOL_GROUPS * POOL_CH
MLSTM_CONV = 4
FFN_CONV = 3
D_FF = 2816
EPS = 1e-6
LOG2E = 1.4426950408889634

LANES = 128
SUBLANES = 8
GATE_W = LANES
QK_HALO = SUBLANES
POOL_HALO = 2 * SUBLANES
FFN_HALO = SUBLANES

TM_IN = 1024
TM_FFN = 512
MLSTM_L = 256
MLSTM_NB = 2
SB_TQ = 256
SB_TK = 256
SB_DEAD = 160.0
FFN_FC = 256
VMEM_LIMIT = 56 * 1024 * 1024


def _rms(x, g):
    return x * lax.rsqrt(jnp.mean(x * x, axis=-1, keepdims=True) + EPS) * g


def _sigmoid(x):
    return 1.0 / (1.0 + jnp.exp(-x))


def _log_sigmoid(x):
    return jnp.minimum(x, 0.0) - jnp.log1p(jnp.exp(-jnp.abs(x)))


def _causal_conv(ext, w, b, halo):
    taps = w.shape[0]
    acc = b + w[taps - 1:taps, :] * ext[halo:, :]
    for back in range(1, taps):
        acc = acc + w[taps - 1 - back:taps - back, :] * pltpu.roll(ext, back, 0)[halo:, :]
    return acc


def _gelu_tanh(x):
    k = -2.0 * 0.7978845608028654 * LOG2E
    return x / (1.0 + jnp.exp2(x * (x * x * (k * 0.044715) + k)))


def _const_spec(shape):
    nd = len(shape)
    return pl.BlockSpec(shape, lambda *_: (0,) * nd, pipeline_mode=pl.Buffered(1))


def _layer_spec(arr, layer):
    return pl.BlockSpec((None,) + arr.shape[1:], lambda *_: (layer, 0, 0), pipeline_mode=pl.Buffered(1))


def _inproj_kernel(x_ref, g_ref, wa_ref, wb_ref, wg_ref, cw_ref, cb_ref, gb_ref, pw_ref, ps_ref, pg_ref,
                   q_ref, k_ref, v_ref, o_ref, gate_ref, qsb_ref, ksb_ref, vsb_ref, yp_ref,
                   qk_buf, u_buf, *, tiles_per_seq):
    tm = x_ref.shape[0]
    tile = pl.program_id(0) % tiles_per_seq
    h = _rms(x_ref[...], g_ref[...]).astype(BF16)

    @pl.when(tile == 0)
    def _():
        qk_buf[0:QK_HALO, :] = jnp.zeros((QK_HALO, 2 * W_M), F32)
        u_buf[0:POOL_HALO, :] = jnp.zeros((POOL_HALO, W_P), F32)

    @pl.when(tile != 0)
    def _():
        qk_buf[0:QK_HALO, :] = qk_buf[tm:tm + QK_HALO, :]
        u_buf[0:POOL_HALO, :] = u_buf[tm:tm + POOL_HALO, :]

    qk_buf[QK_HALO:QK_HALO + tm, :] = jnp.dot(h, wa_ref[:, 0:2 * W_M], preferred_element_type=F32)
    v_ref[...] = jnp.dot(h, wa_ref[:, 2 * W_M:3 * W_M], preferred_element_type=F32).astype(BF16)
    o_ref[...] = jnp.dot(h, wa_ref[:, 3 * W_M:4 * W_M], preferred_element_type=F32).astype(BF16)
    gates = jnp.dot(h, wg_ref[...], preferred_element_type=F32) + gb_ref[...]
    sb = jnp.dot(h, wb_ref[:, 0:3 * W_SB], preferred_element_type=F32)

    acc = _causal_conv(qk_buf[...], cw_ref[...], cb_ref[...], QK_HALO)
    qk = acc * _sigmoid(acc)
    q_ref[...] = qk[:, :W_M].astype(BF16)
    k_ref[...] = (qk[:, W_M:] * (MLSTM_HEAD_DIM ** -0.5)).astype(BF16)

    lane = lax.broadcasted_iota(jnp.int32, gates.shape, 1)
    gate_ref[...] = jnp.where(lane < MLSTM_HEADS, gates, _log_sigmoid(gates))

    for hh in range(SB_HEADS):
        lo = hh * SB_HEAD_DIM
        qsb_ref[hh] = (sb[:, lo:lo + SB_HEAD_DIM] * (SB_HEAD_DIM ** -0.5 * LOG2E)).astype(BF16)
        ksb_ref[hh] = sb[:, W_SB + lo:W_SB + lo + SB_HEAD_DIM].astype(BF16)
        vsb_ref[hh] = sb[:, 2 * W_SB + lo:2 * W_SB + lo + SB_HEAD_DIM].astype(BF16)

    u = jnp.dot(h, wb_ref[:, 3 * W_SB:3 * W_SB + W_P], preferred_element_type=F32)
    u_buf[POOL_HALO:POOL_HALO + tm, :] = u
    ext = u_buf[...]
    sums = []
    s = ext
    for shift in (1, 2, 4, 8):
        s = s + pltpu.roll(s, shift, 0)
        sums.append(s[POOL_HALO:, :])
    grp = lax.broadcasted_iota(jnp.int32, (tm, W_P), 1) // POOL_CH
    pos = (tile * tm + lax.broadcasted_iota(jnp.int32, (tm, W_P), 0)).astype(F32)
    win = sums[-1]
    width = jnp.full((tm, W_P), float(POOL_WINDOWS[-1]), F32)
    for gi in range(POOL_GROUPS - 2, -1, -1):
        win = jnp.where(grp == gi, sums[gi], win)
        width = jnp.where(grp == gi, float(POOL_WINDOWS[gi]), width)
    y = win / jnp.minimum(pos + 1.0, width) - u
    yp = jnp.dot(y.astype(BF16), pw_ref[...], preferred_element_type=F32) * ps_ref[...]
    yp_ref[...] = _rms(yp, pg_ref[...]).astype(BF16)


def _inproj(x, *params, layer, seq):
    n = x.shape[0]
    tm = TM_IN
    tok = lambda w: pl.BlockSpec((tm, w), lambda i: (i, 0))
    head = pl.BlockSpec((SB_HEADS, tm, SB_HEAD_DIM), lambda i: (0, i, 0))
    sds = jax.ShapeDtypeStruct
    return pl.pallas_call(
        functools.partial(_inproj_kernel, tiles_per_seq=seq // tm),
        grid=(n // tm,),
        in_specs=[tok(D_MODEL)] + [_layer_spec(p, layer) for p in params],
        out_specs=[tok(W_M), tok(W_M), tok(W_M), tok(W_M), tok(GATE_W), head, head, head, tok(W_P)],
        out_shape=[sds((n, W_M), BF16), sds((n, W_M), BF16), sds((n, W_M), BF16), sds((n, W_M), BF16),
                   sds((n, GATE_W), F32),
                   sds((SB_HEADS, n, SB_HEAD_DIM), BF16), sds((SB_HEADS, n, SB_HEAD_DIM), BF16),
                   sds((SB_HEADS, n, SB_HEAD_DIM), BF16), sds((n, W_P), BF16)],
        scratch_shapes=[pltpu.VMEM((tm + QK_HALO, 2 * W_M), F32), pltpu.VMEM((tm + POOL_HALO, W_P), F32)],
        compiler_params=pltpu.CompilerParams(dimension_semantics=("arbitrary",), vmem_limit_bytes=VMEM_LIMIT),
        name="inproj",
    )(x, *params)


def _split3(x):
    hi = x.astype(BF16)
    r = x - hi.astype(F32)
    mid = r.astype(BF16)
    return hi, mid, (r - mid.astype(F32)).astype(BF16)


def _mlstm_kernel(q_ref, k_ref, v_ref, o_ref, gc_ref, gr_ref, og_ref, tl_ref, tu_ref, y_ref, c_ref, m_ref):
    nb, L, _ = q_ref.shape
    dh = MLSTM_HEAD_DIM
    chains = [(bb, h) for bb in range(nb) for h in range(MLSTM_HEADS)]

    @pl.when(pl.program_id(1) == 0)
    def _():
        c_ref[...] = jnp.zeros(c_ref.shape, F32)
        m_ref[...] = jnp.zeros(m_ref.shape, F32)

    causal = lax.broadcasted_iota(jnp.int32, (L, L), 1) <= lax.broadcasted_iota(jnp.int32, (L, L), 0)
    ones = jnp.ones((L, dh), BF16)
    hs = [slice(h * dh, (h + 1) * dh) for h in range(MLSTM_HEADS)]
    gc, gr, b_cols, b_rows = [], [], [], []
    for bb in range(nb):
        gc.append(gc_ref[bb])
        gr.append(gr_ref[bb, 0])
        b_cols.append(jnp.dot(tl_ref[...], jnp.concatenate(_split3(gc[bb]), axis=0), preferred_element_type=F32))
        b_rows.append(jnp.dot(jnp.concatenate(_split3(gr[bb]), axis=1), tu_ref[...], preferred_element_type=F32))

    qk, qc, v_aug, c_aug = {}, {}, {}, {}
    for ch in chains:
        bb, h = ch
        q = q_ref[bb, :, hs[h]]
        v_aug[ch] = jnp.concatenate([v_ref[bb, :, hs[h]], ones], axis=1)
        c_aug[ch] = c_ref[bb * MLSTM_HEADS + h]
        qk[ch] = lax.dot_general(q, k_ref[bb, :, hs[h]], (((1,), (1,)), ((), ())), preferred_element_type=F32)
        qc[ch] = jnp.dot(q, c_aug[ch].astype(BF16), preferred_element_type=F32)

    hv, cols = {}, {}
    for ch in chains:
        bb, h = ch
        f = MLSTM_HEADS + h
        row = bb * MLSTM_HEADS + h
        b_c = b_cols[bb][:, f:f + 1]
        i_c = gc[bb][:, h:h + 1]
        w_r = gr[bb][h:h + 1, :] - b_rows[bb][f:f + 1, :]
        m_prev = m_ref[row:row + 1, 0:1]
        cm = jnp.max(jnp.where(causal, w_r, -jnp.inf), axis=1, keepdims=True)
        inter = b_c + m_prev
        m_t = jnp.maximum(inter, b_c + cm)
        dexp = jnp.exp2(jnp.where(causal, (b_c - m_t) * LOG2E + w_r * LOG2E, -jnp.inf))
        hv[ch] = jnp.dot((qk[ch] * dexp).astype(BF16), v_aug[ch], preferred_element_type=F32)
        cols[ch] = (b_c, i_c, cm, inter, m_t, m_prev)

    ys = {}
    for ch in chains:
        bb, h = ch
        row = bb * MLSTM_HEADS + h
        b_c, i_c, cm, inter, m_t, m_prev = cols[ch]
        tot = hv[ch] + jnp.exp(inter - m_t) * qc[ch]
        hh = tot[:, :dh] / jnp.maximum(jnp.abs(tot[:, dh:]), jnp.exp(-m_t))
        ys[ch] = _sigmoid(o_ref[bb, :, hs[h]].astype(F32)) * hh
        b_end = b_c[L - 1:L, :]
        m_new = jnp.maximum(b_end + m_prev, b_end + cm[L - 1:L, :])
        decay = jnp.exp(b_end + m_prev - m_new)
        kw = (k_ref[bb, :, hs[h]].astype(F32) * jnp.exp(b_end - b_c + i_c - m_new)).astype(BF16)
        c_ref[row] = decay * c_aug[ch] + lax.dot_general(kw, v_aug[ch], (((0,), (0,)), ((), ())),
                                                         preferred_element_type=F32)
        m_ref[row:row + 1, :] = jnp.broadcast_to(m_new, (1, LANES))
    for bb in range(nb):
        y = jnp.concatenate([ys[(bb, h)] for h in range(MLSTM_HEADS)], axis=1)
        y_ref[bb] = _rms(y, og_ref[...]).astype(BF16)


def _mlstm(q, k, v, o, gates, og, *, layer, batch, seq):
    n = q.shape[0]
    L = MLSTM_L
    nb = MLSTM_NB
    nc = seq // L
    assert batch % nb == 0 and nb * MLSTM_HEADS <= SUBLANES
    seqs = lambda a: a.reshape(batch, seq, a.shape[-1])
    gr = jnp.swapaxes(gates[:, :2 * MLSTM_HEADS].reshape(batch, nc, L, 2 * MLSTM_HEADS), 2, 3)
    lower = (np.arange(L)[None, :] <= np.arange(L)[:, None]).astype(np.float32)
    tl = jnp.asarray(np.concatenate([lower] * 3, axis=1), BF16)
    tu = jnp.asarray(np.concatenate([lower.T] * 3, axis=0), BF16)
    tok = lambda w: pl.BlockSpec((nb, L, w), lambda b, t: (b, t, 0))
    y = pl.pallas_call(
        _mlstm_kernel,
        grid=(batch // nb, nc),
        in_specs=[tok(W_M), tok(W_M), tok(W_M), tok(W_M), tok(GATE_W),
                  pl.BlockSpec((nb, 1, 2 * MLSTM_HEADS, L), lambda b, t: (b, t, 0, 0)),
                  _layer_spec(og, layer), _const_spec(tl.shape), _const_spec(tu.shape)],
        out_specs=tok(W_M),
        out_shape=jax.ShapeDtypeStruct((batch, seq, W_M), BF16),
        scratch_shapes=[pltpu.VMEM((nb * MLSTM_HEADS, MLSTM_HEAD_DIM, 2 * MLSTM_HEAD_DIM), F32),
                        pltpu.VMEM((SUBLANES, LANES), F32)],
        compiler_params=pltpu.CompilerParams(dimension_semantics=("arbitrary", "arbitrary"),
                                             vmem_limit_bytes=VMEM_LIMIT),
        name="mlstm",
    )(seqs(q), seqs(k), seqs(v), seqs(o), seqs(gates), gr, og, tl, tu)
    return y.reshape(n, W_M)


def _sb_kernel(q_ref, k_ref, v_ref, mm_ref, o_ref):
    nh, tq, _ = q_ref.shape
    tk = SB_TK
    i = pl.program_id(1)
    row = lax.broadcasted_iota(jnp.int32, (tq, tk), 0)
    col = lax.broadcasted_iota(jnp.int32, (tq, tk), 1)
    valid = col < row

    def step(j, state, masked):
        start = pl.multiple_of(j * tk, tk)
        zs, lbs, firsts, css, out = [], [], [], [], []
        for h in range(nh):
            kb = k_ref[h, pl.ds(start, tk), :]
            zs.append(lax.dot_general(q_ref[h], kb, (((1,), (1,)), ((), ())), preferred_element_type=F32))
        for h in range(nh):
            z = zs[h]
            nl = jnp.maximum(z, 0.0) + jnp.log(1.0 + jnp.exp2(-jnp.abs(z))) * LOG2E
            lbs.append(z - nl)
            if masked:
                nl = jnp.where(valid, nl, 0.0)
            hi = nl.astype(BF16)
            lo = (nl - hi.astype(F32)).astype(BF16)
            firsts.append(nl[:, 0:1])
            css.append(jnp.dot(jnp.concatenate([hi, lo], axis=1), mm_ref[...], preferred_element_type=F32))
        for h in range(nh):
            carry, acc = state[h]
            after = css[h] + jnp.concatenate([carry] * (tk // LANES), axis=1)
            a = jnp.exp2(lbs[h] - after)
            if masked:
                a = jnp.where(valid, a, 0.0)
            vb = v_ref[h, pl.ds(start, tk), :]
            acc = acc + jnp.dot(a.astype(BF16), vb, preferred_element_type=F32)
            total = css[h][:, 0:1] + firsts[h]
            out.append((carry + jnp.broadcast_to(total, (tq, LANES)), acc))
        return tuple(out)

    def least_carry(st):
        m = st[0][0]
        for h in range(1, nh):
            m = jnp.minimum(m, st[h][0])
        return jnp.min(m)

    def cond(c):
        return jnp.logical_and(c[0] >= 0, c[1] < SB_DEAD)

    def body(c):
        st = step(c[0], c[2], False)
        return c[0] - 1, least_carry(st), st

    state = tuple((jnp.zeros((tq, LANES), F32), jnp.zeros((tq, SB_HEAD_DIM), F32)) for _ in range(nh))
    state = step(i, state, True)
    state = lax.while_loop(cond, body, (i - 1, least_carry(state), state))[2]
    for h in range(nh):
        o_ref[h] = state[h][1].astype(BF16)


def _sb_matrix():
    tk = SB_TK
    later = (np.arange(tk)[:, None] > np.arange(tk)[None, :]).astype(np.float32)
    return jnp.asarray(np.concatenate([later, later], axis=0), BF16)


def _sb(q, k, v, *, batch, seq):
    n = q.shape[1]
    assert SB_TQ == SB_TK
    nq = seq // SB_TQ
    mm = _sb_matrix()
    return pl.pallas_call(
        _sb_kernel,
        grid=(batch, nq),
        in_specs=[pl.BlockSpec((SB_HEADS, SB_TQ, SB_HEAD_DIM), lambda b, i: (0, b * nq + i, 0)),
                  pl.BlockSpec((SB_HEADS, seq, SB_HEAD_DIM), lambda b, i: (0, b, 0)),
                  pl.BlockSpec((SB_HEADS, seq, SB_HEAD_DIM), lambda b, i: (0, b, 0)),
                  pl.BlockSpec(mm.shape, lambda b, i: (0, 0))],
        out_specs=pl.BlockSpec((SB_HEADS, SB_TQ, SB_HEAD_DIM), lambda b, i: (0, b * nq + i, 0)),
        out_shape=jax.ShapeDtypeStruct((SB_HEADS, n, SB_HEAD_DIM), BF16),
        compiler_params=pltpu.CompilerParams(dimension_semantics=("arbitrary", "arbitrary"),
                                             vmem_limit_bytes=VMEM_LIMIT),
        name="stickbreak",
    )(q, k, v, mm)


def _outffn_kernel(x_ref, ym_ref, ysb_ref, yp_ref, sbg_ref, wo_ref, pmg_ref, pfg_ref,
                   wu_ref, cw_ref, cb_ref, wd_ref, pog_ref,
                   out_ref, h_buf, act_buf, up_buf, halo_buf, *, tiles_per_seq):
    tm = x_ref.shape[0]
    fc = FFN_FC
    nchunk = D_FF // fc
    tile = pl.program_id(0) % tiles_per_seq

    ysb = jnp.concatenate([ysb_ref[hh].astype(F32) for hh in range(SB_HEADS)], axis=1)
    mix = jnp.concatenate([ym_ref[...], _rms(ysb, sbg_ref[...]).astype(BF16), yp_ref[...]], axis=1)
    x1 = x_ref[...] + _rms(jnp.dot(mix, wo_ref[...], preferred_element_type=F32), pmg_ref[...])
    h_buf[...] = _rms(x1, pfg_ref[...]).astype(BF16)

    @pl.when(tile == 0)
    def _():
        halo_buf[...] = jnp.zeros(halo_buf.shape, F32)

    def cols(c, half):
        lo = half * D_FF + c * fc
        return slice(lo, lo + fc)

    def up_dots(c):
        hb = h_buf[...]
        return tuple(jnp.dot(hb, wu_ref[:, cols(c, half)], preferred_element_type=F32) for half in (0, 1))

    def conv_half(c, half, up):
        buf = up_buf.at[2 * (c % 2) + half]
        buf[0:FFN_HALO, :] = halo_buf[2 * c + half]
        buf[FFN_HALO:FFN_HALO + tm, :] = up
        halo_buf[2 * c + half] = up[tm - FFN_HALO:, :]
        return _causal_conv(buf[...], cw_ref[:, cols(c, half)], cb_ref[:, cols(c, half)], FFN_HALO)

    ups = up_dots(0)
    for c in range(nchunk):
        nxt = up_dots(c + 1) if c + 1 < nchunk else None
        gate = conv_half(c, 0, ups[0])
        val = conv_half(c, 1, ups[1])
        act_buf[c] = (_gelu_tanh(gate) * val).astype(BF16)
        ups = nxt
    ffn = jnp.dot(act_buf[0], wd_ref[0:fc, :], preferred_element_type=F32)
    for c in range(1, nchunk):
        ffn = ffn + jnp.dot(act_buf[c], wd_ref[c * fc:(c + 1) * fc, :], preferred_element_type=F32)
    out_ref[...] = x1 + _rms(ffn, pog_ref[...])


def _outffn(x, ym, ysb, yp, *params, layer, seq):
    n = x.shape[0]
    tm = TM_FFN
    fc = FFN_FC
    nchunk = D_FF // fc
    tok = lambda w: pl.BlockSpec((tm, w), lambda i: (i, 0))
    return pl.pallas_call(
        functools.partial(_outffn_kernel, tiles_per_seq=seq // tm),
        grid=(n // tm,),
        in_specs=[tok(D_MODEL), tok(W_M), pl.BlockSpec((SB_HEADS, tm, SB_HEAD_DIM), lambda i: (0, i, 0)), tok(W_P)]
                 + [_layer_spec(p, layer) for p in params],
        out_specs=tok(D_MODEL),
        out_shape=jax.ShapeDtypeStruct((n, D_MODEL), F32),
        scratch_shapes=[pltpu.VMEM((tm, D_MODEL), BF16), pltpu.VMEM((nchunk, tm, fc), BF16),
                        pltpu.VMEM((4, tm + FFN_HALO, fc), F32), pltpu.VMEM((2 * nchunk, FFN_HALO, fc), F32)],
        compiler_params=pltpu.CompilerParams(dimension_semantics=("arbitrary",), vmem_limit_bytes=VMEM_LIMIT),
        name="outffn",
    )(x, ym, ysb, yp, *params)


def kernel(x, pre_mix_g, w_in, mlstm_conv_w, mlstm_conv_b, i_bias, f_bias, pool_w, pool_scale, mlstm_out_g,
           sb_out_g, pool_out_g, w_out, post_mix_g, pre_ffn_g, ffn_w_up, ffn_conv_w, ffn_conv_b, ffn_w_down,
           post_ffn_g):
    batch, seq, d_model = x.shape
    depth = w_in.shape[0]
    assert d_model == D_MODEL and ffn_w_down.shape[1] == D_FF
    assert seq % TM_IN == 0 and seq % TM_FFN == 0 and seq % MLSTM_L == 0 and seq % SB_TQ == 0
    n = batch * seq
    xf = x.reshape(n, d_model)
    rows = lambda a: a.reshape(depth, 1, -1).astype(F32)
    o_gate = 4 * W_M
    o_sb = o_gate + 2 * MLSTM_HEADS
    pad_gate = GATE_W - 2 * MLSTM_HEADS
    wa = w_in[:, :, :o_gate].astype(BF16)
    wb = w_in[:, :, o_sb:].astype(BF16)
    wg = jnp.pad(w_in[:, :, o_gate:o_sb], ((0, 0), (0, 0), (0, pad_gate))).astype(BF16)
    gb = jnp.pad(jnp.concatenate([i_bias, f_bias], axis=1), ((0, 0), (0, pad_gate))).reshape(depth, 1, GATE_W)
    eye = jnp.eye(POOL_GROUPS, dtype=F32)
    pw = (pool_w[:, :, :, None, :] * eye[None, :, None, :, None]).reshape(depth, W_P, W_P).astype(BF16)
    in_params = (rows(pre_mix_g), wa, wb, wg, mlstm_conv_w, rows(mlstm_conv_b), gb, pw, rows(pool_scale),
                 rows(pool_out_g))
    out_params = (rows(sb_out_g), w_out.astype(BF16), rows(post_mix_g), rows(pre_ffn_g), ffn_w_up.astype(BF16),
                  ffn_conv_w, rows(ffn_conv_b), ffn_w_down.astype(BF16), rows(post_ffn_g))
    og = rows(mlstm_out_g)
    for l in range(depth):
        q, k, v, o, gates, qsb, ksb, vsb, yp = _inproj(xf, *in_params, layer=l, seq=seq)
        ym = _mlstm(q, k, v, o, gates, og, layer=l, batch=batch, seq=seq)
        ysb = _sb(qsb, ksb, vsb, batch=batch, seq=seq)
        xf = _outffn(xf, ym, ysb, yp, *out_params, layer=l, seq=seq)
    return xf.reshape(batch, seq, d_model)
```
